```python
import functools
import jax, jax.numpy as jnp
from jax import lax
import numpy as np

D_MODEL = 4096
BATCH = 4
SEQ = 2048
DEPTH = 2
DEC_BATCH = 128
DEC_SEQ = 4
PAST_LEN = 16384
PAGE_SIZE = 128

RET_HEADS = 8
RET_DK = D_MODEL // 16
RET_DV = D_MODEL // 8
RET_QK = RET_HEADS * RET_DK
RET_V = RET_HEADS * RET_DV
SSD_HEADDIM = 64
SSD_HEADS = D_MODEL // SSD_HEADDIM
SSD_INNER = SSD_HEADS * SSD_HEADDIM
SSD_STATE = 128
SSD_GROUPS = 8
SSD_HPG = SSD_HEADS // SSD_GROUPS
SSD_CONV = 4
SSD_XBC = SSD_INNER + 2 * SSD_GROUPS * SSD_STATE
MLSTM_HEADS = 8
MLSTM_DK = D_MODEL // 16
MLSTM_DV = D_MODEL // 8
MLSTM_QK = MLSTM_HEADS * MLSTM_DK
MLSTM_V = MLSTM_HEADS * MLSTM_DV
PEER_HEADS = 8
PEER_NKEYS = 128
PEER_EXPERTS = PEER_NKEYS * PEER_NKEYS
PEER_DKEY = 256
PEER_TOPK = 16
PEER_BLOCK = 64
PLE_DIM = 256
CHUNK = 128
ROPE_BASE = 10000.0
EPS = 1e-6

IN_SIZES = (RET_QK, RET_QK, RET_V, RET_V,
            SSD_INNER, SSD_XBC, SSD_HEADS,
            MLSTM_QK, MLSTM_QK, MLSTM_V, MLSTM_V, MLSTM_HEADS, MLSTM_HEADS,
            3 * D_MODEL)
IN_OFFSETS = tuple(int(v) for v in np.cumsum(IN_SIZES)[:-1])
IN_TOTAL = int(sum(IN_SIZES))

kernel_name = 'hybrid_ret_ssd_mlstm_peer_step'

F32 = jnp.float32


def rmsnorm(x, gain):
    xf = x.astype(F32)
    y = xf * lax.rsqrt(jnp.mean(xf * xf, axis=-1, keepdims=True) + EPS)
    return (y * gain.astype(F32)).astype(x.dtype)


def head_norm(o, gain):
    c = o - jnp.mean(o, axis=-1, keepdims=True)
    y = c * lax.rsqrt(jnp.mean(c * c, axis=-1, keepdims=True) + EPS)
    b, t, h, e = o.shape
    return y.reshape(b, t, h * e) * gain.astype(F32)


def rotary(x, pos):
    d = x.shape[-1]
    inv = ROPE_BASE ** (-jnp.arange(0, d, 2, dtype=F32) / d)
    ang = pos[:, None] * inv[None, :]
    cos = jnp.cos(ang)[None, :, None, :]
    sin = jnp.sin(ang)[None, :, None, :]
    x1, x2 = x[..., : d // 2], x[..., d // 2:]
    return jnp.concatenate([x1 * cos - x2 * sin, x1 * sin + x2 * cos], axis=-1)


def run_chunks(step, state, xs, chunk):
    b, t = xs[0].shape[:2]
    if t <= chunk:
        return step(state, xs)
    n = t // chunk
    xs_c = tuple(jnp.moveaxis(a.reshape((b, n, chunk) + a.shape[2:]), 1, 0) for a in xs)
    state, ys = lax.scan(step, state, xs_c)
    ys = jnp.moveaxis(ys, 0, 1)
    return state, ys.reshape((b, t) + ys.shape[3:])


def retention_step(log_gamma, state, xs):
    q, k, v = xs
    L = q.shape[1]
    idx = jnp.arange(L, dtype=F32)
    diff = idx[:, None] - idx[None, :]
    causal = diff >= 0
    decay = jnp.where(causal[None],
                      jnp.exp(jnp.where(causal, diff, 0.0)[None] * log_gamma[:, None, None]), 0.0)
    scores = jnp.einsum('blhd,bshd->bhls', q, k) * decay[None]
    inner = jnp.einsum('bhls,bshe->blhe', scores, v)
    w_from_state = jnp.exp((idx[:, None] + 1.0) * log_gamma[None, :])
    cross = jnp.einsum('blhd,bhde->blhe', q, state) * w_from_state[None, :, :, None]
    w_to_state = jnp.exp((L - 1.0 - idx)[:, None] * log_gamma[None, :])
    new_state = (state * jnp.exp(L * log_gamma)[None, :, None, None]
                 + jnp.einsum('bshd,sh,bshe->bhde', k, w_to_state, v))
    return new_state, inner + cross


def retention_branch(rq, rk, rv, rg, gn_gain, pos, state):
    b, t, _ = rq.shape
    q = rotary(rq.reshape(b, t, RET_HEADS, RET_DK).astype(F32), pos)
    k = rotary(rk.reshape(b, t, RET_HEADS, RET_DK).astype(F32), pos) * (RET_DK ** -0.5)
    v = rv.reshape(b, t, RET_HEADS, RET_DV).astype(F32)
    log_gamma = jnp.log1p(-jnp.exp2(-5.0 - jnp.arange(RET_HEADS, dtype=F32)))
    state, o = run_chunks(functools.partial(retention_step, log_gamma), state.astype(F32), (q, k, v), CHUNK)
    y = jax.nn.silu(rg.astype(F32)) * head_norm(o, gn_gain)
    return y.astype(rq.dtype), state


def causal_dwconv(xpad, w, bias):
    c = xpad.shape[-1]
    out = lax.conv_general_dilated(xpad, w[:, None, :].astype(xpad.dtype), (1,), 'VALID',
                                   dimension_numbers=('NWC', 'WIO', 'NWC'), feature_group_count=c)
    return out + bias.astype(xpad.dtype)


def ssd_step(state, xs):
    x, dt, a, bm, cm = xs
    L = x.shape[1]
    acum = jnp.cumsum(a, axis=1)
    seg = acum[:, :, None] - acum[:, None, :]
    causal = (jnp.arange(L)[:, None] >= jnp.arange(L)[None, :])[None, :, :, None, None]
    decay = jnp.where(causal, jnp.exp(jnp.where(causal, seg, 0.0)), 0.0)
    cb = jnp.einsum('btgn,bsgn->btsg', cm, bm)
    w = decay * cb[..., None] * dt[:, None]
    y_intra = jnp.einsum('btsge,bsgep->btgep', w, x)
    y_inter = jnp.einsum('btgn,bgepn->btgep', cm, state) * jnp.exp(acum)[..., None]
    wl = jnp.exp(acum[:, -1:] - acum) * dt
    new_state = (state * jnp.exp(acum[:, -1])[..., None, None]
                 + jnp.einsum('bsge,bsgep,bsgn->bgepn', wl, x, bm))
    return new_state, y_intra + y_inter


def ssd_branch(sz, sxbc, sdt, conv_w, conv_b, dt_bias, a_log, d_skip, norm_gain, conv_state, ssm_state):
    b, t, _ = sxbc.shape
    xpad = jnp.concatenate([conv_state.astype(sxbc.dtype), sxbc], axis=1)
    new_conv = xpad[:, xpad.shape[1] - (SSD_CONV - 1):]
    xc = jax.nn.silu(causal_dwconv(xpad, conv_w, conv_b).astype(F32))
    x = xc[..., :SSD_INNER].reshape(b, t, SSD_GROUPS, SSD_HPG, SSD_HEADDIM)
    bm = xc[..., SSD_INNER:SSD_INNER + SSD_GROUPS * SSD_STATE].reshape(b, t, SSD_GROUPS, SSD_STATE)
    cm = xc[..., SSD_INNER + SSD_GROUPS * SSD_STATE:].reshape(b, t, SSD_GROUPS, SSD_STATE)
    dt = jax.nn.softplus(sdt.astype(F32) + dt_bias.astype(F32)).reshape(b, t, SSD_GROUPS, SSD_HPG)
    a = dt * (-jnp.exp(a_log.astype(F32))).reshape(SSD_GROUPS, SSD_HPG)
    h0 = ssm_state.astype(F32).reshape(b, SSD_GROUPS, SSD_HPG, SSD_HEADDIM, SSD_STATE)
    h, y = run_chunks(ssd_step, h0, (x, dt, a, bm, cm), CHUNK)
    y = y + x * d_skip.astype(F32).reshape(SSD_GROUPS, SSD_HPG)[..., None]
    y = y.reshape(b, t, SSD_INNER) * jax.nn.silu(sz.astype(F32))
    y = rmsnorm(y, norm_gain)
    return y.astype(sxbc.dtype), new_conv, h.reshape(b, SSD_HEADS, SSD_HEADDIM, SSD_STATE)


def mlstm_step(state, xs):
    c0, n0, m0 = state
    q, k, v, ig, lf = xs
    L = q.shape[1]
    fcum = jnp.cumsum(lf, axis=1)
    causal = (jnp.arange(L)[:, None] >= jnp.arange(L)[None, :])[None, :, :, None]
    dlog = jnp.where(causal, fcum[:, :, None] - fcum[:, None, :] + ig[:, None], -jnp.inf)
    b_log = fcum + m0[:, None]
    m_t = jnp.maximum(b_log, jnp.max(dlog, axis=2))
    dm = jnp.exp(dlog - m_t[:, :, None])
    inter = jnp.exp(b_log - m_t)
    s = jnp.einsum('bthd,bshd->btsh', q, k) * dm
    num = jnp.einsum('btsh,bshe->bthe', s, v) + inter[..., None] * jnp.einsum('bthd,bhde->bthe', q, c0)
    den = jnp.sum(s, axis=2) + inter * jnp.einsum('bthd,bhd->bth', q, n0)
    h = num / jnp.maximum(jnp.abs(den), jnp.exp(-m_t))[..., None]
    m_new = m_t[:, -1]
    wk = jnp.exp(fcum[:, -1:] - fcum + ig - m_new[:, None])
    carry = jnp.exp(fcum[:, -1] + m0 - m_new)
    c_new = c0 * carry[..., None, None] + jnp.einsum('bsh,bshd,bshe->bhde', wk, k, v)
    n_new = n0 * carry[..., None] + jnp.einsum('bsh,bshd->bhd', wk, k)
    return (c_new, n_new, m_new), h


def mlstm_branch(mq, mk, mv, mo, mi, mf, b_i, b_f, norm_gain, c0, n0, m0):
    b, t, _ = mq.shape
    q = mq.reshape(b, t, MLSTM_HEADS, MLSTM_DK).astype(F32)
    k = mk.reshape(b, t, MLSTM_HEADS, MLSTM_DK).astype(F32) * (MLSTM_DK ** -0.5)
    v = mv.reshape(b, t, MLSTM_HEADS, MLSTM_DV).astype(F32)
    ig = mi.astype(F32) + b_i.astype(F32)
    lf = jax.nn.log_sigmoid(mf.astype(F32) + b_f.astype(F32))
    state0 = (c0.astype(F32), n0.astype(F32), m0.astype(F32))
    (c1, n1, m1), h = run_chunks(mlstm_step, state0, (q, k, v, ig, lf), CHUNK)
    y = jax.nn.sigmoid(mo.astype(F32)) * head_norm(h, norm_gain)
    return y.astype(mq.dtype), c1, n1, m1


def peer(u, w_q, keys1, keys2, exp_u, exp_v):
    b, t, d = u.shape
    ntok = b * t
    xt = u.reshape(ntok, d)
    q = (xt @ w_q).astype(F32).reshape(ntok, PEER_HEADS, PEER_DKEY)
    half = PEER_DKEY // 2
    s1 = jnp.einsum('thd,nd->thn', q[..., :half], keys1.astype(F32))
    s2 = jnp.einsum('thd,nd->thn', q[..., half:], keys2.astype(F32))
    v1, i1 = lax.top_k(s1, PEER_TOPK)
    v2, i2 = lax.top_k(s2, PEER_TOPK)
    cand = (v1[..., :, None] + v2[..., None, :]).reshape(ntok, PEER_HEADS, PEER_TOPK * PEER_TOPK)
    cidx = (i1[..., :, None] * PEER_NKEYS + i2[..., None, :]).reshape(ntok, PEER_HEADS, PEER_TOPK * PEER_TOPK)
    sc, pick = lax.top_k(cand, PEER_TOPK)
    eidx = jnp.take_along_axis(cidx, pick, axis=-1).reshape(ntok, PEER_HEADS * PEER_TOPK)
    gate = jax.nn.softmax(sc, axis=-1).reshape(ntok, PEER_HEADS * PEER_TOPK)
    nb = -(-ntok // PEER_BLOCK)
    pad = nb * PEER_BLOCK - ntok
    xb = jnp.pad(xt, ((0, pad), (0, 0))).reshape(nb, PEER_BLOCK, d)
    eb = jnp.pad(eidx, ((0, pad), (0, 0))).reshape(nb, PEER_BLOCK, PEER_HEADS * PEER_TOPK)
    gb = jnp.pad(gate, ((0, pad), (0, 0))).reshape(nb, PEER_BLOCK, PEER_HEADS * PEER_TOPK)

    def block(args):
        xs, es, gs = args
        act = jax.nn.gelu(jnp.einsum('td,tkd->tk', xs, exp_u[es]).astype(F32), approximate=False)
        return jnp.einsum('tk,tkd->td', (gs * act).astype(xs.dtype), exp_v[es])

    out = lax.map(block, (xb, eb, gb)).reshape(nb * PEER_BLOCK, d)[:ntok]
    return out.reshape(b, t, d)


def run_group(x, pe, pos, s_ret, s_ssm, s_conv, s_c, s_n, s_m,
              norm_mix, w_in, ret_norm_gain, ssd_conv_w, ssd_conv_b, ssd_dt_bias, ssd_a_log, ssd_d,
              ssd_norm_gain, mlstm_b_i, mlstm_b_f, mlstm_norm_gain, w_branch_ret, w_branch_ssd,
              w_branch_mlstm, w_out, norm_ffn, peer_w_q, peer_keys1, peer_keys2, peer_u, peer_v,
              norm_ple, w_ple_gate, w_ple, norm_final):
    h = x
    o_ret, o_ssm, o_conv, o_c, o_n, o_m = [], [], [], [], [], []
    for l in range(DEPTH):
        u = rmsnorm(h, norm_mix[l])
        proj = u @ w_in[l]
        (rq, rk, rv, rg, sz, sxbc, sdt, mq, mk, mv, mo, mi, mf, gates) = jnp.split(proj, IN_OFFSETS, axis=-1)
        y_r, ns_r = retention_branch(rq, rk, rv, rg, ret_norm_gain[l], pos, s_ret[l])
        y_s, ns_conv, ns_ssm = ssd_branch(sz, sxbc, sdt, ssd_conv_w[l], ssd_conv_b[l], ssd_dt_bias[l],
                                          ssd_a_log[l], ssd_d[l], ssd_norm_gain[l], s_conv[l], s_ssm[l])
        y_m, ns_c, ns_n, ns_m = mlstm_branch(mq, mk, mv, mo, mi, mf, mlstm_b_i[l], mlstm_b_f[l],
                                             mlstm_norm_gain[l], s_c[l], s_n[l], s_m[l])
        g = jax.nn.sigmoid(gates.astype(F32))
        g_r, g_s, g_m = g[..., :D_MODEL], g[..., D_MODEL:2 * D_MODEL], g[..., 2 * D_MODEL:]
        merged = (g_r * (y_r @ w_branch_ret[l]).astype(F32)
                  + g_s * (y_s @ w_branch_ssd[l]).astype(F32)
                  + g_m * (y_m @ w_branch_mlstm[l]).astype(F32)).astype(x.dtype)
        h = h + merged @ w_out[l]
        h = h + peer(rmsnorm(h, norm_ffn[l]), peer_w_q[l], peer_keys1[l], peer_keys2[l], peer_u[l], peer_v[l])
        gate_ple = jax.nn.sigmoid((rmsnorm(h, norm_ple[l]) @ w_ple_gate[l]).astype(F32))
        h = h + (gate_ple * (pe[l] @ w_ple[l]).astype(F32)).astype(x.dtype)
        o_ret.append(ns_r); o_ssm.append(ns_ssm); o_conv.append(ns_conv)
        o_c.append(ns_c); o_n.append(ns_n); o_m.append(ns_m)
    y = rmsnorm(h, norm_final)
    return (y, jnp.stack(o_ret), jnp.stack(o_ssm), jnp.stack(o_conv),
            jnp.stack(o_c), jnp.stack(o_n), jnp.stack(o_m))


def setup_inputs(seed: int = 0) -> dict:
    key = jax.random.key(seed)
    ks = iter(jax.random.split(key, 64))

    def nrm(shape, scale):
        return scale * jax.random.normal(next(ks), shape, F32)

    def gain(shape):
        return 1.0 + 0.05 * jax.random.normal(next(ks), shape, F32)

    dt0 = jnp.exp(jax.random.uniform(next(ks), (DEPTH, SSD_HEADS), F32, float(np.log(1e-3)), float(np.log(1e-1))))
    return {
        'x_prompt': nrm((BATCH, SEQ, D_MODEL), 1.0),
        'x_sample': nrm((DEC_BATCH, DEC_SEQ, D_MODEL), 1.0),
        'state_ret': nrm((DEPTH, DEC_BATCH, RET_HEADS, RET_DK, RET_DV), 0.1),
        'state_ssm': nrm((DEPTH, DEC_BATCH, SSD_HEADS, SSD_HEADDIM, SSD_STATE), 0.1),
        'state_conv': nrm((DEPTH, DEC_BATCH, SSD_CONV - 1, SSD_XBC), 1.0),
        'state_mlstm_c': nrm((DEPTH, DEC_BATCH, MLSTM_HEADS, MLSTM_DK, MLSTM_DV), 0.1),
        'state_mlstm_n': nrm((DEPTH, DEC_BATCH, MLSTM_HEADS, MLSTM_DK), 0.1),
        'state_mlstm_m': nrm((DEPTH, DEC_BATCH, MLSTM_HEADS), 1.0),
        'p_prompt': nrm((DEPTH, BATCH, SEQ, PLE_DIM), 1.0),
        'p_sample': nrm((DEPTH, DEC_BATCH, DEC_SEQ, PLE_DIM), 1.0),
        'norm_mix': gain((DEPTH, D_MODEL)),
        'w_in': nrm((DEPTH, D_MODEL, IN_TOTAL), D_MODEL ** -0.5),
        'ret_norm_gain': gain((DEPTH, RET_V)),
        'ssd_conv_w': nrm((DEPTH, SSD_CONV, SSD_XBC), 0.5),
        'ssd_conv_b': nrm((DEPTH, SSD_XBC), 0.02),
        'ssd_dt_bias': dt0 + jnp.log(-jnp.expm1(-dt0)),
        'ssd_a_log': jnp.log(jax.random.uniform(next(ks), (DEPTH, SSD_HEADS), F32, 1.0, 16.0)),
        'ssd_d': gain((DEPTH, SSD_HEADS)),
        'ssd_norm_gain': gain((DEPTH, SSD_INNER)),
        'mlstm_b_i': nrm((DEPTH, MLSTM_HEADS), 0.1),
        'mlstm_b_f': 3.0 + nrm((DEPTH, MLSTM_HEADS), 0.5),
        'mlstm_norm_gain': gain((DEPTH, MLSTM_V)),
        'w_branch_ret': nrm((DEPTH, RET_V, D_MODEL), RET_V ** -0.5),
        'w_branch_ssd': nrm((DEPTH, SSD_INNER, D_MODEL), SSD_INNER ** -0.5),
        'w_branch_mlstm': nrm((DEPTH, MLSTM_V, D_MODEL), MLSTM_V ** -0.5),
        'w_out': nrm((DEPTH, D_MODEL, D_MODEL), D_MODEL ** -0.5),
        'norm_ffn': gain((DEPTH, D_MODEL)),
        'peer_w_q': nrm((DEPTH, D_MODEL, PEER_HEADS * PEER_DKEY), D_MODEL ** -0.5),
        'peer_keys1': nrm((DEPTH, PEER_NKEYS, PEER_DKEY // 2), (PEER_DKEY // 2) ** -0.5),
        'peer_keys2': nrm((DEPTH, PEER_NKEYS, PEER_DKEY // 2), (PEER_DKEY // 2) ** -0.5),
        'peer_u': nrm((DEPTH, PEER_EXPERTS, D_MODEL), D_MODEL ** -0.5),
        'peer_v': nrm((DEPTH, PEER_EXPERTS, D_MODEL), PEER_HEADS ** -0.5),
        'norm_ple': gain((DEPTH, D_MODEL)),
        'w_ple_gate': nrm((DEPTH, D_MODEL, D_MODEL), D_MODEL ** -0.5),
        'w_ple': nrm((DEPTH, PLE_DIM, D_MODEL), PLE_DIM ** -0.5),
        'norm_final': gain((D_MODEL,)),
    }


def reference(x_prompt, x_sample, state_ret, state_ssm, state_conv, state_mlstm_c, state_mlstm_n,
              state_mlstm_m, p_prompt, p_sample, norm_mix, w_in, ret_norm_gain, ssd_conv_w, ssd_conv_b,
              ssd_dt_bias, ssd_a_log, ssd_d, ssd_norm_gain, mlstm_b_i, mlstm_b_f, mlstm_norm_gain,
              w_branch_ret, w_branch_ssd, w_branch_mlstm, w_out, norm_ffn, peer_w_q, peer_keys1,
              peer_keys2, peer_u, peer_v, norm_ple, w_ple_gate, w_ple, norm_final):
    weights = (norm_mix, w_in, ret_norm_gain, ssd_conv_w, ssd_conv_b, ssd_dt_bias, ssd_a_log, ssd_d,
               ssd_norm_gain, mlstm_b_i, mlstm_b_f, mlstm_norm_gain, w_branch_ret, w_branch_ssd,
               w_branch_mlstm, w_out, norm_ffn, peer_w_q, peer_keys1, peer_keys2, peer_u, peer_v,
               norm_ple, w_ple_gate, w_ple, norm_final)
    bp, tp = x_prompt.shape[0], x_prompt.shape[1]
    z_ret = jnp.zeros((DEPTH, bp, RET_HEADS, RET_DK, RET_DV), F32)
    z_ssm = jnp.zeros((DEPTH, bp, SSD_HEADS, SSD_HEADDIM, SSD_STATE), F32)
    z_conv = jnp.zeros((DEPTH, bp, SSD_CONV - 1, SSD_XBC), x_prompt.dtype)
    z_c = jnp.zeros((DEPTH, bp, MLSTM_HEADS, MLSTM_DK, MLSTM_DV), F32)
    z_n = jnp.zeros((DEPTH, bp, MLSTM_HEADS, MLSTM_DK), F32)
    z_m = jnp.full((DEPTH, bp, MLSTM_HEADS), -jnp.inf, F32)
    pos_p = jnp.arange(tp, dtype=F32)
    pos_s = PAST_LEN + jnp.arange(x_sample.shape[1], dtype=F32)
    (y_prompt, p_ret, p_ssm, p_conv, p_c, p_n, p_m) = run_group(
        x_prompt, p_prompt, pos_p, z_ret, z_ssm, z_conv, z_c, z_n, z_m, *weights)
    (y_sample, s_ret, s_ssm, s_conv, s_c, s_n, s_m) = run_group(
        x_sample, p_sample, pos_s, state_ret, state_ssm, state_conv, state_mlstm_c, state_mlstm_n,
        state_mlstm_m, *weights)
    return (y_prompt, y_sample, p_ret, p_ssm, p_conv, p_c, p_n, p_m, s_ret, s_ssm, s_conv, s_c, s_n, s_m)
```

```python
import functools

import numpy as np
import jax
import jax.numpy as jnp
from jax import lax
from jax.experimental import pallas as pl
from jax.experimental.pallas import tpu as pltpu

D_MODEL = 4096
DEPTH = 2
PAST_LEN = 16384

RET_HEADS = 8
RET_DK = D_MODEL // 16
RET_DV = D_MODEL // 8
RET_QK = RET_HEADS * RET_DK
RET_V = RET_HEADS * RET_DV
SSD_HEADDIM = 64
SSD_HEADS = D_MODEL // SSD_HEADDIM
SSD_INNER = SSD_HEADS * SSD_HEADDIM
SSD_STATE = 128
SSD_GROUPS = 8
SSD_HPG = SSD_HEADS // SSD_GROUPS
SSD_CONV = 4
SSD_XBC = SSD_INNER + 2 * SSD_GROUPS * SSD_STATE
MLSTM_HEADS = 8
MLSTM_DK = D_MODEL // 16
MLSTM_DV = D_MODEL // 8
MLSTM_QK = MLSTM_HEADS * MLSTM_DK
MLSTM_V = MLSTM_HEADS * MLSTM_DV
PEER_HEADS = 8
PEER_NKEYS = 128
PEER_DKEY = 256
PEER_TOPK = 16
CHUNK = 128
ROPE_BASE = 10000.0
EPS = 1e-6

IN_SIZES = (RET_QK, RET_QK, RET_V, RET_V,
            SSD_INNER, SSD_XBC, SSD_HEADS,
            MLSTM_QK, MLSTM_QK, MLSTM_V, MLSTM_V, MLSTM_HEADS, MLSTM_HEADS,
            3 * D_MODEL)
IN_OFFSETS = tuple(int(v) for v in np.cumsum((0,) + IN_SIZES))

F32 = jnp.float32
BF16 = jnp.bfloat16

V7X_VMEM_BYTES = 64 * 1024 * 1024
VMEM_LIMIT = V7X_VMEM_BYTES - 8 * 1024 * 1024
LANES = 128


def _params(*semantics):
    return pltpu.CompilerParams(dimension_semantics=semantics, vmem_limit_bytes=VMEM_LIMIT)


def _rmsnorm_kernel(*refs, n_in, emit_sum):
    xs = refs[:n_in]
    g_ref = refs[n_in]
    outs = refs[n_in + 1:]
    x = xs[0][...]
    for r in xs[1:]:
        x = x + r[...]
    y = x * lax.rsqrt(jnp.mean(x * x, axis=-1, keepdims=True) + EPS) * g_ref[...]
    outs[0][...] = y.astype(outs[0].dtype)
    if emit_sum:
        outs[1][...] = x


def rmsnorm(xs, gain, out_dtype, emit_sum=False, tm=256):
    t, d = xs[0].shape
    row = pl.BlockSpec((tm, d), lambda i: (i, 0))
    out_shape = [jax.ShapeDtypeStruct((t, d), out_dtype)]
    out_specs = [row]
    if emit_sum:
        out_shape.append(jax.ShapeDtypeStruct((t, d), F32))
        out_specs.append(row)
    res = pl.pallas_call(
        functools.partial(_rmsnorm_kernel, n_in=len(xs), emit_sum=emit_sum),
        grid=(t // tm,),
        in_specs=[row] * len(xs) + [pl.BlockSpec((1, d), lambda i: (0, 0))],
        out_specs=out_specs,
        out_shape=out_shape,
        compiler_params=_params("parallel"),
        name="rmsnorm",
    )(*xs, gain.reshape(1, d).astype(F32))
    return res if emit_sum else res[0]


def _matmul_kernel(a_ref, b_ref, *refs, epilogue):
    *extra, o_ref = refs
    acc = jnp.dot(a_ref[...], b_ref[...], preferred_element_type=F32)
    if epilogue is not None:
        acc = epilogue(acc, *[e[...] for e in extra])
    o_ref[...] = acc.astype(o_ref.dtype)


def matmul(a, b, out_dtype=F32, epilogue=None, extras=(), tm=512, tn=1024, name="matmul"):
    m, k = a.shape
    _, n = b.shape
    tn = min(tn, n)
    tm = min(tm, m)
    in_specs = [pl.BlockSpec((tm, k), lambda j, i: (i, 0)),
                pl.BlockSpec((k, tn), lambda j, i: (0, j))]
    for _, off in extras:
        in_specs.append(pl.BlockSpec((tm, tn), lambda j, i, off=off: (i, j + off)))
    return pl.pallas_call(
        functools.partial(_matmul_kernel, epilogue=epilogue),
        grid=(n // tn, m // tm),
        in_specs=in_specs,
        out_specs=pl.BlockSpec((tm, tn), lambda j, i: (i, j)),
        out_shape=jax.ShapeDtypeStruct((m, n), out_dtype),
        compiler_params=_params("parallel", "parallel"),
        name=name,
    )(a, b, *[e for e, _ in extras])


def _gated_epilogue(acc, gate):
    return jax.nn.sigmoid(gate) * acc


def _gated_add_epilogue(acc, gate, prev):
    return prev + jax.nn.sigmoid(gate) * acc


def _residual_epilogue(acc, res):
    return res + acc


def _ple_epilogue(acc, h, ple):
    return h + jax.nn.sigmoid(acc) * ple


def _top_values(s, k):
    n = s.shape[0]
    row = lax.broadcasted_iota(jnp.int32, s.shape, 0)
    vals = []
    for _ in range(k):
        m = jnp.max(s, axis=0, keepdims=True)
        first = jnp.min(jnp.where(s == m, row, n), axis=0, keepdims=True)
        s = jnp.where(row == first, -jnp.inf, s)
        vals.append(m)
    return vals


def _peer_scores_kernel(q_ref, k1_ref, k2_ref, s1_ref, s2_ref, e1_ref, e2_ref, thr_ref):
    half = PEER_DKEY // 2
    contract_last = (((1,), (1,)), ((), ()))
    thr_rows = []
    for h in range(PEER_HEADS):
        q1 = q_ref[:, h * PEER_DKEY: h * PEER_DKEY + half]
        q2 = q_ref[:, h * PEER_DKEY + half: (h + 1) * PEER_DKEY]
        s1 = lax.dot_general(k1_ref[...], q1, contract_last, precision=lax.Precision.HIGHEST,
                             preferred_element_type=F32)
        s2 = lax.dot_general(k2_ref[...], q2, contract_last, precision=lax.Precision.HIGHEST,
                             preferred_element_type=F32)
        v1 = _top_values(s1, PEER_TOPK)
        v2 = _top_values(s2, PEER_TOPK)
        v2_all = jnp.concatenate(v2, axis=0)
        cand = jnp.concatenate([v1[a] + v2_all for a in range(PEER_TOPK)], axis=0)
        sc = _top_values(cand, PEER_TOPK)
        top = sc[0]
        z = jnp.exp(sc[0] - top)
        for c in sc[1:]:
            z = z + jnp.exp(c - top)
        s1_ref[h] = s1
        s2_ref[h] = s2
        e1_ref[h] = jnp.exp(s1 - v1[0])
        e2_ref[h] = jnp.exp(s2 - v2[0]) / z
        thr_rows.append(sc[PEER_TOPK - 1])
    thr_ref[...] = jnp.concatenate(thr_rows, axis=0)


def peer_scores(q, keys1, keys2, tb=256):
    t = q.shape[0]
    big = jax.ShapeDtypeStruct((PEER_HEADS, PEER_NKEYS, t), F32)
    big_spec = pl.BlockSpec((PEER_HEADS, PEER_NKEYS, tb), lambda i: (0, 0, i))
    key_spec = pl.BlockSpec((PEER_NKEYS, PEER_DKEY // 2), lambda i: (0, 0))
    return pl.pallas_call(
        _peer_scores_kernel,
        grid=(t // tb,),
        in_specs=[pl.BlockSpec((tb, PEER_HEADS * PEER_DKEY), lambda i: (i, 0)), key_spec, key_spec],
        out_specs=[big_spec] * 4 + [pl.BlockSpec((PEER_HEADS, tb), lambda i: (0, i))],
        out_shape=[big] * 4 + [jax.ShapeDtypeStruct((PEER_HEADS, t), F32)],
        compiler_params=_params("parallel"),
        name="peer_scores",
    )(q, keys1.astype(F32), keys2.astype(F32))


def _peer_dense_kernel(x_ref, u_ref, v_ref, s1_ref, s2_ref, e1_ref, e2_ref, thr_ref, o_ref, *, rows):
    j = pl.program_id(1)

    @pl.when(j == 0)
    def _():
        o_ref[...] = jnp.zeros_like(o_ref)

    contract_last = (((1,), (1,)), ((), ()))
    hid = lax.dot_general(u_ref[...], x_ref[...], contract_last, preferred_element_type=F32)
    act = 0.5 * hid * (1.0 + lax.erf(hid * (0.5 ** 0.5)))
    pieces = []
    for r in range(rows):
        i = j * rows + r
        gate = None
        for h in range(PEER_HEADS):
            s1 = s1_ref[h, pl.ds(i, 1), :]
            e1 = e1_ref[h, pl.ds(i, 1), :]
            keep = (s1 + s2_ref[h]) >= thr_ref[pl.ds(h, 1), :]
            term = jnp.where(keep, e1 * e2_ref[h], 0.0)
            gate = term if gate is None else gate + term
        pieces.append(gate * act[r * PEER_NKEYS:(r + 1) * PEER_NKEYS])
    weighted = jnp.concatenate(pieces, axis=0) if rows > 1 else pieces[0]
    o_ref[...] += jnp.dot(weighted.T.astype(BF16), v_ref[...], preferred_element_type=F32)


def peer_dense(x, u, v, s1, s2, e1, e2, thr, tb=512, eb=256):
    t, d = x.shape
    n_exp = u.shape[0]
    rows = eb // PEER_NKEYS
    tok_spec = pl.BlockSpec((PEER_HEADS, PEER_NKEYS, tb), lambda i, j: (0, 0, i),
                            pipeline_mode=pl.Buffered(1))
    return pl.pallas_call(
        functools.partial(_peer_dense_kernel, rows=rows),
        grid=(t // tb, n_exp // eb),
        in_specs=[pl.BlockSpec((tb, d), lambda i, j: (i, 0), pipeline_mode=pl.Buffered(1)),
                  pl.BlockSpec((eb, d), lambda i, j: (j, 0)),
                  pl.BlockSpec((eb, d), lambda i, j: (j, 0)),
                  tok_spec, tok_spec, tok_spec, tok_spec,
                  pl.BlockSpec((PEER_HEADS, tb), lambda i, j: (0, i), pipeline_mode=pl.Buffered(1))],
        out_specs=pl.BlockSpec((tb, d), lambda i, j: (i, 0)),
        out_shape=jax.ShapeDtypeStruct((t, d), F32),
        compiler_params=_params("parallel", "arbitrary"),
        name="peer_dense",
    )(x, u, v, s1, s2, e1, e2, thr)


def head_norm(o, gain):
    c = o - jnp.mean(o, axis=-1, keepdims=True)
    y = c * lax.rsqrt(jnp.mean(c * c, axis=-1, keepdims=True) + EPS)
    b, t, h, e = o.shape
    return y.reshape(b, t, h * e) * gain.astype(F32)


def rotary(x, pos):
    d = x.shape[-1]
    inv = ROPE_BASE ** (-jnp.arange(0, d, 2, dtype=F32) / d)
    ang = pos[:, None] * inv[None, :]
    cos = jnp.cos(ang)[None, :, None, :]
    sin = jnp.sin(ang)[None, :, None, :]
    x1, x2 = x[..., : d // 2], x[..., d // 2:]
    return jnp.concatenate([x1 * cos - x2 * sin, x1 * sin + x2 * cos], axis=-1)


def run_chunks(step, state, xs, chunk):
    b, t = xs[0].shape[:2]
    if t <= chunk:
        return step(state, xs)
    n = t // chunk
    xs_c = tuple(jnp.moveaxis(a.reshape((b, n, chunk) + a.shape[2:]), 1, 0) for a in xs)
    state, ys = lax.scan(step, state, xs_c)
    ys = jnp.moveaxis(ys, 0, 1)
    return state, ys.reshape((b, t) + ys.shape[3:])


def retention_step(log_gamma, state, xs):
    q, k, v = xs
    L = q.shape[1]
    idx = jnp.arange(L, dtype=F32)
    diff = idx[:, None] - idx[None, :]
    causal = diff >= 0
    decay = jnp.where(causal[None],
                      jnp.exp(jnp.where(causal, diff, 0.0)[None] * log_gamma[:, None, None]), 0.0)
    scores = jnp.einsum('blhd,bshd->bhls', q, k) * decay[None]
    inner = jnp.einsum('bhls,bshe->blhe', scores, v)
    w_from_state = jnp.exp((idx[:, None] + 1.0) * log_gamma[None, :])
    cross = jnp.einsum('blhd,bhde->blhe', q, state) * w_from_state[None, :, :, None]
    w_to_state = jnp.exp((L - 1.0 - idx)[:, None] * log_gamma[None, :])
    new_state = (state * jnp.exp(L * log_gamma)[None, :, None, None]
                 + jnp.einsum('bshd,sh,bshe->bhde', k, w_to_state, v))
    return new_state, inner + cross


def retention_branch(rq, rk, rv, rg, gn_gain, pos, state):
    b, t, _ = rq.shape
    q = rotary(rq.reshape(b, t, RET_HEADS, RET_DK), pos)
    k = rotary(rk.reshape(b, t, RET_HEADS, RET_DK), pos) * (RET_DK ** -0.5)
    v = rv.reshape(b, t, RET_HEADS, RET_DV)
    log_gamma = jnp.log1p(-jnp.exp2(-5.0 - jnp.arange(RET_HEADS, dtype=F32)))
    state, o = run_chunks(functools.partial(retention_step, log_gamma), state, (q, k, v), CHUNK)
    y = jax.nn.silu(rg) * head_norm(o, gn_gain)
    return y, state


def ssd_step(state, xs):
    x, dt, a, bm, cm = xs
    L = x.shape[1]
    acum = jnp.cumsum(a, axis=1)
    seg = acum[:, :, None] - acum[:, None, :]
    causal = (jnp.arange(L)[:, None] >= jnp.arange(L)[None, :])[None, :, :, None, None]
    decay = jnp.where(causal, jnp.exp(jnp.where(causal, seg, 0.0)), 0.0)
    cb = jnp.einsum('btgn,bsgn->btsg', cm, bm)
    w = decay * cb[..., None] * dt[:, None]
    y_intra = jnp.einsum('btsge,bsgep->btgep', w, x)
    y_inter = jnp.einsum('btgn,bgepn->btgep', cm, state) * jnp.exp(acum)[..., None]
    wl = jnp.exp(acum[:, -1:] - acum) * dt
    new_state = (state * jnp.exp(acum[:, -1])[..., None, None]
                 + jnp.einsum('bsge,bsgep,bsgn->bgepn', wl, x, bm))
    return new_state, y_intra + y_inter


def ssd_branch(sz, sxbc, sdt, conv_w, conv_b, dt_bias, a_log, d_skip, norm_gain, conv_state, ssm_state):
    b, t, _ = sxbc.shape
    xpad = jnp.concatenate([conv_state.astype(sxbc.dtype), sxbc], axis=1)
    new_conv = xpad[:, xpad.shape[1] - (SSD_CONV - 1):]
    conv = conv_b.astype(F32)
    for i in range(SSD_CONV):
        conv = conv + xpad[:, i:i + t] * conv_w[i].astype(F32)
    xc = jax.nn.silu(conv)
    x = xc[..., :SSD_INNER].reshape(b, t, SSD_GROUPS, SSD_HPG, SSD_HEADDIM)
    bm = xc[..., SSD_INNER:SSD_INNER + SSD_GROUPS * SSD_STATE].reshape(b, t, SSD_GROUPS, SSD_STATE)
    cm = xc[..., SSD_INNER + SSD_GROUPS * SSD_STATE:].reshape(b, t, SSD_GROUPS, SSD_STATE)
    dt = jax.nn.softplus(sdt + dt_bias.astype(F32)).reshape(b, t, SSD_GROUPS, SSD_HPG)
    a = dt * (-jnp.exp(a_log.astype(F32))).reshape(SSD_GROUPS, SSD_HPG)
    h0 = ssm_state.astype(F32).reshape(b, SSD_GROUPS, SSD_HPG, SSD_HEADDIM, SSD_STATE)
    h, y = run_chunks(ssd_step, h0, (x, dt, a, bm, cm), CHUNK)
    y = y + x * d_skip.astype(F32).reshape(SSD_GROUPS, SSD_HPG)[..., None]
    y = y.reshape(b, t, SSD_INNER) * jax.nn.silu(sz)
    y = y * lax.rsqrt(jnp.mean(y * y, axis=-1, keepdims=True) + EPS) * norm_gain.astype(F32)
    return y, new_conv, h.reshape(b, SSD_HEADS, SSD_HEADDIM, SSD_STATE)


def mlstm_step(state, xs):
    c0, n0, m0 = state
    q, k, v, ig, lf = xs
    L = q.shape[1]
    fcum = jnp.cumsum(lf, axis=1)
    causal = (jnp.arange(L)[:, None] >= jnp.arange(L)[None, :])[None, :, :, None]
    dlog = jnp.where(causal, fcum[:, :, None] - fcum[:, None, :] + ig[:, None], -jnp.inf)
    b_log = fcum + m0[:, None]
    m_t = jnp.maximum(b_log, jnp.max(dlog, axis=2))
    dm = jnp.exp(dlog - m_t[:, :, None])
    inter = jnp.exp(b_log - m_t)
    s = jnp.einsum('bthd,bshd->btsh', q, k) * dm
    num = jnp.einsum('btsh,bshe->bthe', s, v) + inter[..., None] * jnp.einsum('bthd,bhde->bthe', q, c0)
    den = jnp.sum(s, axis=2) + inter * jnp.einsum('bthd,bhd->bth', q, n0)
    h = num / jnp.maximum(jnp.abs(den), jnp.exp(-m_t))[..., None]
    m_new = m_t[:, -1]
    wk = jnp.exp(fcum[:, -1:] - fcum + ig - m_new[:, None])
    carry = jnp.exp(fcum[:, -1] + m0 - m_new)
    c_new = c0 * carry[..., None, None] + jnp.einsum('bsh,bshd,bshe->bhde', wk, k, v)
    n_new = n0 * carry[..., None] + jnp.einsum('bsh,bshd->bhd', wk, k)
    return (c_new, n_new, m_new), h


def mlstm_branch(mq, mk, mv, mo, mi, mf, b_i, b_f, norm_gain, c0, n0, m0):
    b, t, _ = mq.shape
    q = mq.reshape(b, t, MLSTM_HEADS, MLSTM_DK)
    k = mk.reshape(b, t, MLSTM_HEADS, MLSTM_DK) * (MLSTM_DK ** -0.5)
    v = mv.reshape(b, t, MLSTM_HEADS, MLSTM_DV)
    ig = mi + b_i.astype(F32)
    lf = jax.nn.log_sigmoid(mf + b_f.astype(F32))
    (c1, n1, m1), h = run_chunks(mlstm_step, (c0, n0, m0), (q, k, v, ig, lf), CHUNK)
    y = jax.nn.sigmoid(mo) * head_norm(h, norm_gain)
    return y, c1, n1, m1


def _col_slice(w, name_idx_lo, name_idx_hi):
    return w[:, IN_OFFSETS[name_idx_lo]:IN_OFFSETS[name_idx_hi]].astype(BF16)


def kernel(x_prompt, x_sample, state_ret, state_ssm, state_conv, state_mlstm_c, state_mlstm_n, state_mlstm_m, p_prompt, p_sample, norm_mix, w_in, ret_norm_gain, ssd_conv_w, ssd_conv_b, ssd_dt_bias, ssd_a_log, ssd_d, ssd_norm_gain, mlstm_b_i, mlstm_b_f, mlstm_norm_gain, w_branch_ret, w_branch_ssd, w_branch_mlstm, w_out, norm_ffn, peer_w_q, peer_keys1, peer_keys2, peer_u, peer_v, norm_ple, w_ple_gate, w_ple, norm_final):
    bp, tp, d = x_prompt.shape
    bs, ts, _ = x_sample.shape
    n_p = bp * tp
    n_s = bs * ts
    h = jnp.concatenate([x_prompt.reshape(n_p, d), x_sample.reshape(n_s, d)], axis=0)

    pos_p = jnp.arange(tp, dtype=F32)
    pos_s = PAST_LEN + jnp.arange(ts, dtype=F32)
    groups = ((0, n_p, bp, tp, pos_p), (n_p, n_s, bs, ts, pos_s))

    new_states = [[[] for _ in range(6)] for _ in groups]
    for l in range(DEPTH):
        init_states = (
            (jnp.zeros((bp, RET_HEADS, RET_DK, RET_DV), F32),
             jnp.zeros((bp, SSD_HEADS, SSD_HEADDIM, SSD_STATE), F32),
             jnp.zeros((bp, SSD_CONV - 1, SSD_XBC), F32),
             jnp.zeros((bp, MLSTM_HEADS, MLSTM_DK, MLSTM_DV), F32),
             jnp.zeros((bp, MLSTM_HEADS, MLSTM_DK), F32),
             jnp.full((bp, MLSTM_HEADS), -jnp.inf, F32)),
            (state_ret[l], state_ssm[l], state_conv[l], state_mlstm_c[l], state_mlstm_n[l],
             state_mlstm_m[l]))

        u = rmsnorm([h], norm_mix[l], BF16)
        w = w_in[l]
        seg = [matmul(u, _col_slice(w, i, i + 1), name="in_proj")
               if IN_SIZES[i] % LANES == 0 else None for i in range(len(IN_SIZES))]
        small_w = jnp.concatenate([w[:, IN_OFFSETS[6]:IN_OFFSETS[7]], w[:, IN_OFFSETS[11]:IN_OFFSETS[13]]],
                                  axis=1)
        n_small = small_w.shape[1]
        small_w = jnp.pad(small_w, ((0, 0), (0, LANES - n_small))).astype(BF16)
        small = matmul(u, small_w, name="in_proj_small")
        rq, rk, rv, rg, sz, sxbc = seg[0:6]
        mq, mk, mv, mo = seg[7:11]
        gates = seg[13]
        sdt = small[:, :SSD_HEADS]
        mi = small[:, SSD_HEADS:SSD_HEADS + MLSTM_HEADS]
        mf = small[:, SSD_HEADS + MLSTM_HEADS:SSD_HEADS + 2 * MLSTM_HEADS]

        y_r, y_s, y_m = [], [], []
        for gi, (lo, n, b, t, pos) in enumerate(groups):
            s_ret, s_ssm, s_conv, s_c, s_n, s_m = init_states[gi]

            def grp(a):
                return a[lo:lo + n].reshape(b, t, a.shape[-1])

            yr, ns_r = retention_branch(grp(rq), grp(rk), grp(rv), grp(rg), ret_norm_gain[l], pos, s_ret)
            ys, ns_conv, ns_ssm = ssd_branch(grp(sz), grp(sxbc), grp(sdt), ssd_conv_w[l], ssd_conv_b[l],
                                             ssd_dt_bias[l], ssd_a_log[l], ssd_d[l], ssd_norm_gain[l],
                                             s_conv, s_ssm)
            ym, ns_c, ns_n, ns_m = mlstm_branch(grp(mq), grp(mk), grp(mv), grp(mo), grp(mi), grp(mf),
                                                mlstm_b_i[l], mlstm_b_f[l], mlstm_norm_gain[l],
                                                s_c, s_n, s_m)
            y_r.append(yr.reshape(n, d)); y_s.append(ys.reshape(n, d)); y_m.append(ym.reshape(n, d))
            for slot, val in zip(new_states[gi], (ns_r, ns_ssm, ns_conv, ns_c, ns_n, ns_m)):
                slot.append(val)
        y_r = jnp.concatenate(y_r, axis=0).astype(BF16)
        y_s = jnp.concatenate(y_s, axis=0).astype(BF16)
        y_m = jnp.concatenate(y_m, axis=0).astype(BF16)

        gate_cols = D_MODEL // 1024
        merged = matmul(y_r, w_branch_ret[l].astype(BF16), epilogue=_gated_epilogue,
                        extras=((gates, 0),), name="branch_ret")
        merged = matmul(y_s, w_branch_ssd[l].astype(BF16), epilogue=_gated_add_epilogue,
                        extras=((gates, gate_cols), (merged, 0)), name="branch_ssd")
        merged = matmul(y_m, w_branch_mlstm[l].astype(BF16), out_dtype=BF16, epilogue=_gated_add_epilogue,
                        extras=((gates, 2 * gate_cols), (merged, 0)), name="branch_mlstm")
        h = matmul(merged, w_out[l].astype(BF16), epilogue=_residual_epilogue, extras=((h, 0),),
                   name="w_out")

        uf = rmsnorm([h], norm_ffn[l], BF16)
        q = matmul(uf, peer_w_q[l].astype(BF16), name="peer_q")
        s1, s2, e1, e2, thr = peer_scores(q, peer_keys1[l], peer_keys2[l])
        peer_out = peer_dense(uf, peer_u[l].astype(BF16), peer_v[l].astype(BF16), s1, s2, e1, e2, thr)

        up, h = rmsnorm([h, peer_out], norm_ple[l], BF16, emit_sum=True)
        pe = jnp.concatenate([p_prompt[l].reshape(n_p, -1), p_sample[l].reshape(n_s, -1)], axis=0)
        ple = matmul(pe.astype(BF16), w_ple[l].astype(BF16), name="ple")
        h = matmul(up, w_ple_gate[l].astype(BF16), epilogue=_ple_epilogue, extras=((h, 0), (ple, 0)),
                   name="ple_gate")

    y = rmsnorm([h], norm_final, F32)
    y_prompt = y[:n_p].reshape(bp, tp, d)
    y_sample = y[n_p:].reshape(bs, ts, d)
    outs = [y_prompt, y_sample]
    for gi in range(2):
        outs.extend(jnp.stack(slot) for slot in new_states[gi])
    return tuple(outs)
```

```python
import functools

import numpy as np
import jax
import jax.numpy as jnp
from jax import lax
from jax.experimental import pallas as pl
from jax.experimental.pallas import tpu as pltpu

D_MODEL = 4096
DEPTH = 2
PAST_LEN = 16384

RET_HEADS = 8
RET_DK = D_MODEL // 16
RET_DV = D_MODEL // 8
RET_QK = RET_HEADS * RET_DK
RET_V = RET_HEADS * RET_DV
SSD_HEADDIM = 64
SSD_HEADS = D_MODEL // SSD_HEADDIM
SSD_INNER = SSD_HEADS * SSD_HEADDIM
SSD_STATE = 128
SSD_GROUPS = 8
SSD_HPG = SSD_HEADS // SSD_GROUPS
SSD_CONV = 4
SSD_XBC = SSD_INNER + 2 * SSD_GROUPS * SSD_STATE
MLSTM_HEADS = 8
MLSTM_DK = D_MODEL // 16
MLSTM_DV = D_MODEL // 8
MLSTM_QK = MLSTM_HEADS * MLSTM_DK
MLSTM_V = MLSTM_HEADS * MLSTM_DV
PEER_HEADS = 8
PEER_NKEYS = 128
PEER_DKEY = 256
PEER_TOPK = 16
CHUNK = 128
ROPE_BASE = 10000.0
EPS = 1e-6

IN_SIZES = (RET_QK, RET_QK, RET_V, RET_V,
            SSD_INNER, SSD_XBC, SSD_HEADS,
            MLSTM_QK, MLSTM_QK, MLSTM_V, MLSTM_V, MLSTM_HEADS, MLSTM_HEADS,
            3 * D_MODEL)
IN_OFFSETS = tuple(int(v) for v in np.cumsum((0,) + IN_SIZES))

F32 = jnp.float32
BF16 = jnp.bfloat16

V7X_VMEM_BYTES = 64 * 1024 * 1024
VMEM_LIMIT = V7X_VMEM_BYTES - 8 * 1024 * 1024
LANES = 128


def _params(*semantics):
    return pltpu.CompilerParams(dimension_semantics=semantics, vmem_limit_bytes=VMEM_LIMIT)


def _rmsnorm_kernel(*refs, n_in, emit_sum):
    xs = refs[:n_in]
    g_ref = refs[n_in]
    outs = refs[n_in + 1:]
    x = xs[0][...]
    for r in xs[1:]:
        x = x + r[...]
    y = x * lax.rsqrt(jnp.mean(x * x, axis=-1, keepdims=True) + EPS) * g_ref[...]
    outs[0][...] = y.astype(outs[0].dtype)
    if emit_sum:
        outs[1][...] = x


def rmsnorm(xs, gain, out_dtype, emit_sum=False, tm=256):
    t, d = xs[0].shape
    row = pl.BlockSpec((tm, d), lambda i: (i, 0))
    out_shape = [jax.ShapeDtypeStruct((t, d), out_dtype)]
    out_specs = [row]
    if emit_sum:
        out_shape.append(jax.ShapeDtypeStruct((t, d), F32))
        out_specs.append(row)
    res = pl.pallas_call(
        functools.partial(_rmsnorm_kernel, n_in=len(xs), emit_sum=emit_sum),
        grid=(t // tm,),
        in_specs=[row] * len(xs) + [pl.BlockSpec((1, d), lambda i: (0, 0))],
        out_specs=out_specs,
        out_shape=out_shape,
        compiler_params=_params("parallel"),
        name="rmsnorm",
    )(*xs, gain.reshape(1, d).astype(F32))
    return res if emit_sum else res[0]


def _matmul_kernel(a_ref, b_ref, *refs, epilogue):
    *extra, o_ref = refs
    acc = jnp.dot(a_ref[...], b_ref[...], preferred_element_type=F32)
    if epilogue is not None:
        acc = epilogue(acc, *[e[...] for e in extra])
    o_ref[...] = acc.astype(o_ref.dtype)


def matmul(a, b, out_dtype=F32, epilogue=None, extras=(), tm=512, tn=1024, name="matmul"):
    m, k = a.shape
    _, n = b.shape
    tn = min(tn, n)
    tm = min(tm, m)
    in_specs = [pl.BlockSpec((tm, k), lambda j, i: (i, 0)),
                pl.BlockSpec((k, tn), lambda j, i: (0, j))]
    for _, off in extras:
        in_specs.append(pl.BlockSpec((tm, tn), lambda j, i, off=off: (i, j + off)))
    return pl.pallas_call(
        functools.partial(_matmul_kernel, epilogue=epilogue),
        grid=(n // tn, m // tm),
        in_specs=in_specs,
        out_specs=pl.BlockSpec((tm, tn), lambda j, i: (i, j)),
        out_shape=jax.ShapeDtypeStruct((m, n), out_dtype),
        compiler_params=_params("parallel", "parallel"),
        name=name,
    )(a, b, *[e for e, _ in extras])


def _gated_epilogue(acc, gate):
    return jax.nn.sigmoid(gate) * acc


def _gated_add_epilogue(acc, gate, prev):
    return prev + jax.nn.sigmoid(gate) * acc


def _residual_epilogue(acc, res):
    return res + acc


def _ple_epilogue(acc, h, ple):
    return h + jax.nn.sigmoid(acc) * ple


def _top_values(s, k):
    n = s.shape[0]
    row = lax.broadcasted_iota(jnp.int32, s.shape, 0)
    vals = []
    for _ in range(k):
        m = jnp.max(s, axis=0, keepdims=True)
        first = jnp.min(jnp.where(s == m, row, n), axis=0, keepdims=True)
        s = jnp.where(row == first, -jnp.inf, s)
        vals.append(m)
    return vals


def _peer_scores_kernel(q_ref, k1_ref, k2_ref, s1_ref, s2_ref, e1_ref, e2_ref, thr_ref):
    half = PEER_DKEY // 2
    contract_last = (((1,), (1,)), ((), ()))
    thr_rows = []
    for h in range(PEER_HEADS):
        q1 = q_ref[:, h * PEER_DKEY: h * PEER_DKEY + half]
        q2 = q_ref[:, h * PEER_DKEY + half: (h + 1) * PEER_DKEY]
        s1 = lax.dot_general(k1_ref[...], q1, contract_last, precision=lax.Precision.HIGHEST,
                             preferred_element_type=F32)
        s2 = lax.dot_general(k2_ref[...], q2, contract_last, precision=lax.Precision.HIGHEST,
                             preferred_element_type=F32)
        v1 = _top_values(s1, PEER_TOPK)
        v2 = _top_values(s2, PEER_TOPK)
        v1_all = jnp.concatenate(v1, axis=0)
        v2_all = jnp.concatenate(v2, axis=0)
        sub = PEER_TOPK // 2
        cand = jnp.concatenate([v1[0] + v2_all] + [v1[a] + v2_all[:sub] for a in range(1, sub)]
                               + [v1_all[sub:] + v2[0]], axis=0)
        sc = _top_values(cand, PEER_TOPK)
        top = sc[0]
        z = jnp.exp(sc[0] - top)
        for c in sc[1:]:
            z = z + jnp.exp(c - top)
        s1_ref[h] = s1
        s2_ref[h] = s2
        e1_ref[h] = jnp.exp(s1 - v1[0])
        e2_ref[h] = jnp.exp(s2 - v2[0]) / z
        thr_rows.append(sc[PEER_TOPK - 1])
    thr_ref[...] = jnp.concatenate(thr_rows, axis=0)


def peer_scores(q, keys1, keys2, tb=256):
    t = q.shape[0]
    big = jax.ShapeDtypeStruct((PEER_HEADS, PEER_NKEYS, t), F32)
    big_spec = pl.BlockSpec((PEER_HEADS, PEER_NKEYS, tb), lambda i: (0, 0, i))
    key_spec = pl.BlockSpec((PEER_NKEYS, PEER_DKEY // 2), lambda i: (0, 0))
    return pl.pallas_call(
        _peer_scores_kernel,
        grid=(t // tb,),
        in_specs=[pl.BlockSpec((tb, PEER_HEADS * PEER_DKEY), lambda i: (i, 0)), key_spec, key_spec],
        out_specs=[big_spec] * 4 + [pl.BlockSpec((PEER_HEADS, tb), lambda i: (0, i))],
        out_shape=[big] * 4 + [jax.ShapeDtypeStruct((PEER_HEADS, t), F32)],
        compiler_params=_params("parallel"),
        name="peer_scores",
    )(q, keys1.astype(F32), keys2.astype(F32))


def _peer_dense_kernel(x_ref, u_ref, v_ref, s1_ref, s2_ref, e1_ref, e2_ref, thr_ref, o_ref, *, rows):
    j = pl.program_id(1)

    @pl.when(j == 0)
    def _():
        o_ref[...] = jnp.zeros_like(o_ref)

    contract_last = (((1,), (1,)), ((), ()))
    hid = lax.dot_general(u_ref[...], x_ref[...], contract_last, preferred_element_type=F32)
    act = 0.5 * hid * (1.0 + lax.erf(hid * (0.5 ** 0.5)))
    pieces = []
    for r in range(rows):
        i = j * rows + r
        gate = None
        for h in range(PEER_HEADS):
            s1 = s1_ref[h, pl.ds(i, 1), :]
            e1 = e1_ref[h, pl.ds(i, 1), :]
            keep = (s1 + s2_ref[h]) >= thr_ref[pl.ds(h, 1), :]
            term = jnp.where(keep, e1 * e2_ref[h], 0.0)
            gate = term if gate is None else gate + term
        pieces.append(gate * act[r * PEER_NKEYS:(r + 1) * PEER_NKEYS])
    weighted = jnp.concatenate(pieces, axis=0) if rows > 1 else pieces[0]
    o_ref[...] += jnp.dot(weighted.T.astype(BF16), v_ref[...], preferred_element_type=F32)


def peer_dense(x, u, v, s1, s2, e1, e2, thr, tb=512, eb=256):
    t, d = x.shape
    n_exp = u.shape[0]
    rows = eb // PEER_NKEYS
    tok_spec = pl.BlockSpec((PEER_HEADS, PEER_NKEYS, tb), lambda i, j: (0, 0, i),
                            pipeline_mode=pl.Buffered(1))
    return pl.pallas_call(
        functools.partial(_peer_dense_kernel, rows=rows),
        grid=(t // tb, n_exp // eb),
        in_specs=[pl.BlockSpec((tb, d), lambda i, j: (i, 0), pipeline_mode=pl.Buffered(1)),
                  pl.BlockSpec((eb, d), lambda i, j: (j, 0)),
                  pl.BlockSpec((eb, d), lambda i, j: (j, 0)),
                  tok_spec, tok_spec, tok_spec, tok_spec,
                  pl.BlockSpec((PEER_HEADS, tb), lambda i, j: (0, i), pipeline_mode=pl.Buffered(1))],
        out_specs=pl.BlockSpec((tb, d), lambda i, j: (i, 0)),
        out_shape=jax.ShapeDtypeStruct((t, d), F32),
        compiler_params=_params("parallel", "arbitrary"),
        name="peer_dense",
    )(x, u, v, s1, s2, e1, e2, thr)


NT_DIMS = (((1,), (1,)), ((), ()))
TN_DIMS = (((0,), (0,)), ((), ()))


def _dot(a, b):
    return jnp.dot(a.astype(BF16), b.astype(BF16), preferred_element_type=F32)


def _dot_nt(a, b):
    return lax.dot_general(a.astype(BF16), b.astype(BF16), NT_DIMS, preferred_element_type=F32)


def _dot_tn(a, b):
    return lax.dot_general(a.astype(BF16), b.astype(BF16), TN_DIMS, preferred_element_type=F32)


def _rotate(x, cos, sin):
    half = x.shape[1] // 2
    x1, x2 = x[:, :half], x[:, half:]
    return jnp.concatenate([x1 * cos - x2 * sin, x1 * sin + x2 * cos], axis=1)


def _head_norm(o, gain):
    c = o - jnp.mean(o, axis=-1, keepdims=True)
    return c * lax.rsqrt(jnp.mean(c * c, axis=-1, keepdims=True) + EPS) * gain


def _softplus(x):
    return jnp.maximum(x, 0.0) + jnp.log1p(jnp.exp(-jnp.abs(x)))


def _log_sigmoid(x):
    return jnp.minimum(x, 0.0) - jnp.log1p(jnp.exp(-jnp.abs(x)))


def _causal(n):
    t = lax.broadcasted_iota(jnp.int32, (n, n), 0)
    s = lax.broadcasted_iota(jnp.int32, (n, n), 1)
    return t, s, t >= s


def _retention_kernel(q_ref, k_ref, v_ref, g_ref, cos_ref, sin_ref, lg_ref, gain_ref, s0_ref,
                      y_ref, s_out_ref, state, *, n_valid, n_chunks):
    c = pl.program_id(2)
    n = q_ref.shape[0]

    @pl.when(c == 0)
    def _():
        state[...] = s0_ref[0, 0]

    lg = lg_ref[0][:, :1]
    cos, sin = cos_ref[...], sin_ref[...]
    q = _rotate(q_ref[...], cos, sin)
    k = _rotate(k_ref[...], cos, sin) * (RET_DK ** -0.5)
    v = v_ref[...]
    t, s, causal = _causal(n)
    decay = jnp.where(causal, jnp.exp(jnp.where(causal, t - s, 0).astype(F32) * lg), 0.0)
    pos = lax.broadcasted_iota(jnp.int32, (n, 1), 0)
    posf = pos.astype(F32)
    prev = state[...]
    o = _dot(_dot_nt(q, k) * decay, v) + _dot(q, prev) * jnp.exp((posf + 1.0) * lg)
    w_to = jnp.where(pos < n_valid, jnp.exp((n_valid - 1.0 - posf) * lg), 0.0)
    state[...] = prev * jnp.exp(n_valid * lg) + _dot_tn(k * w_to, v)
    y = jax.nn.silu(g_ref[...]) * _head_norm(o, gain_ref[...])
    y_ref[...] = y.astype(y_ref.dtype)

    @pl.when(c == n_chunks - 1)
    def _():
        s_out_ref[0, 0] = state[...]


def retention(rq, rk, rv, rg, cos, sin, gain, s0, *, n_chunks, n, n_valid, out_dtype):
    batch = s0.shape[0]
    log_gamma = jnp.log1p(-jnp.exp2(-5.0 - jnp.arange(RET_HEADS, dtype=F32)))
    lg = jnp.broadcast_to(log_gamma[:, None, None], (RET_HEADS, 1, LANES))

    def tok(width):
        return pl.BlockSpec((n, width), lambda b, h, c: (b * n_chunks + c, h))

    rope = pl.BlockSpec((n, RET_DK // 2), lambda b, h, c: (c, 0))
    st = pl.BlockSpec((1, 1, RET_DK, RET_DV), lambda b, h, c: (b, h, 0, 0))
    return pl.pallas_call(
        functools.partial(_retention_kernel, n_valid=n_valid, n_chunks=n_chunks),
        grid=(batch, RET_HEADS, n_chunks),
        in_specs=[tok(RET_DK), tok(RET_DK), tok(RET_DV), tok(RET_DV), rope, rope,
                  pl.BlockSpec((1, 1, LANES), lambda b, h, c: (h, 0, 0)),
                  pl.BlockSpec((1, RET_DV), lambda b, h, c: (0, h)), st],
        out_specs=[tok(RET_DV), st],
        out_shape=[jax.ShapeDtypeStruct((batch * n_chunks * n, RET_V), out_dtype),
                   jax.ShapeDtypeStruct(s0.shape, F32)],
        scratch_shapes=[pltpu.VMEM((RET_DK, RET_DV), F32)],
        compiler_params=_params("parallel", "parallel", "arbitrary"),
        name="retention",
    )(rq, rk, rv, rg, cos, sin, lg, gain.reshape(1, RET_V).astype(F32), s0)


def _mlstm_kernel(q_ref, k_ref, v_ref, og_ref, small_ref, bias_ref, gain_ref, c0_ref, n0_ref, m0_ref,
                  y_ref, c_out_ref, n_out_ref, m_out_ref, c_s, n_s, m_s, *, n_valid, n_chunks):
    h = pl.program_id(1)
    c = pl.program_id(2)
    n = q_ref.shape[0]

    @pl.when(c == 0)
    def _():
        c_s[...] = c0_ref[0, 0]
        n_s[...] = n0_ref[0, 0]
        m_s[...] = m0_ref[0, 0]

    sm = small_ref[...] + bias_ref[...]
    sm_t = sm.T
    lane = lax.broadcasted_iota(jnp.int32, sm.shape, 1)
    sub = lax.broadcasted_iota(jnp.int32, sm_t.shape, 0)

    def col(idx):
        return jnp.sum(jnp.where(lane == idx, sm, 0.0), axis=1, keepdims=True)

    def row(idx):
        return jnp.sum(jnp.where(sub == idx, sm_t, 0.0), axis=0, keepdims=True)

    i_idx = SSD_HEADS + h
    f_idx = SSD_HEADS + MLSTM_HEADS + h
    ig_col, ig_row = col(i_idx), row(i_idx)
    lf_col, lf_row = _log_sigmoid(col(f_idx)), _log_sigmoid(row(f_idx))
    t, s, causal = _causal(n)
    fcum_col = jnp.sum(jnp.where(causal, lf_row, 0.0), axis=1, keepdims=True)
    fcum_row = jnp.sum(jnp.where(t <= s, lf_col, 0.0), axis=0, keepdims=True)
    dlog = jnp.where(causal, fcum_col - fcum_row + ig_row, -jnp.inf)
    m0 = m_s[:, :1]
    b_log = fcum_col + m0
    m_t = jnp.maximum(b_log, jnp.max(dlog, axis=1, keepdims=True))
    dm = jnp.exp(dlog - m_t)
    inter = jnp.exp(b_log - m_t)

    q = q_ref[...]
    k = k_ref[...] * (MLSTM_DK ** -0.5)
    v = v_ref[...]
    sc = _dot_nt(q, k) * dm
    c_prev = c_s[...]
    n_prev = n_s[...]
    num = _dot(sc, v) + inter * _dot(q, c_prev)
    den = jnp.sum(sc, axis=1, keepdims=True) + inter * jnp.sum(q * n_prev, axis=1, keepdims=True)
    hval = num / jnp.maximum(jnp.abs(den), jnp.exp(-m_t))

    m_new = m_t[n_valid - 1:n_valid, :]
    f_last = fcum_col[n_valid - 1:n_valid, :]
    pos = lax.broadcasted_iota(jnp.int32, (n, 1), 0)
    wk = jnp.where(pos < n_valid, jnp.exp(f_last - fcum_col + ig_col - m_new), 0.0)
    carry = jnp.exp(f_last + m0 - m_new)
    kw = k * wk
    c_s[...] = c_prev * carry + _dot_tn(kw, v)
    n_s[...] = n_prev * carry + jnp.sum(kw, axis=0, keepdims=True)
    m_s[...] = jnp.broadcast_to(m_new, m_s.shape)

    y = jax.nn.sigmoid(og_ref[...]) * _head_norm(hval, gain_ref[...])
    y_ref[...] = y.astype(y_ref.dtype)

    @pl.when(c == n_chunks - 1)
    def _():
        c_out_ref[0, 0] = c_s[...]
        n_out_ref[0, 0] = n_s[...]
        m_out_ref[0, 0] = m_s[...]


def mlstm(mq, mk, mv, mo, small, small_bias, gain, c0, n0, m0, *, n_chunks, n, n_valid, out_dtype):
    batch = c0.shape[0]
    n0 = n0.reshape(batch, MLSTM_HEADS, 1, MLSTM_DK)
    m0 = jnp.broadcast_to(m0[:, :, None, None], (batch, MLSTM_HEADS, 1, LANES))

    def tok(width):
        return pl.BlockSpec((n, width), lambda b, h, c: (b * n_chunks + c, h))

    def st(*tail):
        return pl.BlockSpec((1, 1) + tail, lambda b, h, c: (b, h, 0, 0))

    y, c1, n1, m1 = pl.pallas_call(
        functools.partial(_mlstm_kernel, n_valid=n_valid, n_chunks=n_chunks),
        grid=(batch, MLSTM_HEADS, n_chunks),
        in_specs=[tok(MLSTM_DK), tok(MLSTM_DK), tok(MLSTM_DV), tok(MLSTM_DV),
                  pl.BlockSpec((n, LANES), lambda b, h, c: (b * n_chunks + c, 0)),
                  pl.BlockSpec((1, LANES), lambda b, h, c: (0, 0)),
                  pl.BlockSpec((1, MLSTM_DV), lambda b, h, c: (0, h)),
                  st(MLSTM_DK, MLSTM_DV), st(1, MLSTM_DK), st(1, LANES)],
        out_specs=[tok(MLSTM_DV), st(MLSTM_DK, MLSTM_DV), st(1, MLSTM_DK), st(1, LANES)],
        out_shape=[jax.ShapeDtypeStruct((batch * n_chunks * n, MLSTM_V), out_dtype),
                   jax.ShapeDtypeStruct(c0.shape, F32),
                   jax.ShapeDtypeStruct(n0.shape, F32),
                   jax.ShapeDtypeStruct(m0.shape, F32)],
        scratch_shapes=[pltpu.VMEM((MLSTM_DK, MLSTM_DV), F32), pltpu.VMEM((1, MLSTM_DK), F32),
                        pltpu.VMEM((1, LANES), F32)],
        compiler_params=_params("parallel", "parallel", "arbitrary"),
        name="mlstm",
    )(mq, mk, mv, mo, small, small_bias, gain.reshape(1, MLSTM_V).astype(F32), c0, n0, m0)
    return y, c1, n1.reshape(batch, MLSTM_HEADS, MLSTM_DK), m1[:, :, 0, 0]


SSD_BC = SSD_GROUPS * SSD_STATE
CONV_ROWS = 8


def _ssd_kernel(xbc_ref, z_ref, small_ref, bias_ref, alog_ref, dskip_ref, cw_ref, cb_ref, gain_ref,
                expand_ref, conv0_ref, h0_ref, y_ref, h_out_ref, hstate, xprev, xc_s, ybuf,
                *, n_valid, n_chunks):
    c = pl.program_id(1)
    n = xbc_ref.shape[0]
    pair = 2 * SSD_HEADDIM

    @pl.when(c == 0)
    def _():
        hstate[...] = h0_ref[0]
        xprev[...] = conv0_ref[0]

    cblk = 512
    for j in range(SSD_XBC // cblk):
        cols = slice(j * cblk, (j + 1) * cblk)
        raw = xbc_ref[:, cols]
        ext = jnp.concatenate([xprev[:, cols], raw], axis=0)
        acc = cb_ref[:, cols] + cw_ref[SSD_CONV - 1:SSD_CONV, cols] * raw
        for i in range(SSD_CONV - 1):
            lo = CONV_ROWS - (SSD_CONV - 1) + i
            acc = acc + cw_ref[i:i + 1, cols] * ext[lo:lo + n]
        xc_s[:, cols] = acc * jax.nn.sigmoid(acc)
        xprev[:, cols] = raw[n - CONV_ROWS:n]

    head_lane = lax.broadcasted_iota(jnp.int32, (1, LANES), 1) < SSD_HEADS
    dt = jnp.where(head_lane, _softplus(small_ref[...] + bias_ref[...]), 0.0)
    a = dt * jnp.where(head_lane, -jnp.exp(alog_ref[...]), 0.0)
    t, s, causal = _causal(n)
    acum = jnp.dot(causal.astype(F32), a, precision=lax.Precision.HIGHEST,
                   preferred_element_type=F32)
    pos = lax.broadcasted_iota(jnp.int32, (n, 1), 0)
    a_last = acum[n_valid - 1:n_valid, :]
    wl = jnp.where(pos < n_valid, jnp.exp(a_last - acum) * dt, 0.0)
    acum_t = acum.T
    dt_t = dt.T
    sdec_t = jnp.exp(acum_t[:, n_valid - 1:n_valid])
    expand = expand_ref[...]
    eacum_x = jnp.dot(jnp.exp(acum), expand, precision=lax.Precision.HIGHEST, preferred_element_type=F32)
    wl_x = jnp.dot(wl, expand, precision=lax.Precision.HIGHEST, preferred_element_type=F32)

    first_lane = lax.broadcasted_iota(jnp.int32, (n, pair), 1) < SSD_HEADDIM
    first_row = lax.broadcasted_iota(jnp.int32, (pair, 1), 0) < SSD_HEADDIM
    for g in range(SSD_GROUPS):
        bm = xc_s[:, SSD_INNER + g * SSD_STATE:SSD_INNER + (g + 1) * SSD_STATE]
        cm = xc_s[:, SSD_INNER + SSD_BC + g * SSD_STATE:SSD_INNER + SSD_BC + (g + 1) * SSD_STATE]
        cb = _dot_nt(cm, bm)
        for pr in range(SSD_HPG // 2):
            hd0 = g * SSD_HPG + 2 * pr
            cols = slice(hd0 * SSD_HEADDIM, hd0 * SSD_HEADDIM + pair)
            xp = xc_s[:, cols]
            y_heads = []
            for hd in (hd0, hd0 + 1):
                seg = acum[:, hd:hd + 1] - acum_t[hd:hd + 1, :]
                w = jnp.where(causal, jnp.exp(jnp.where(causal, seg, 0.0)), 0.0) * cb * dt_t[hd:hd + 1, :]
                y_heads.append(_dot(w, xp))
            y_intra = jnp.where(first_lane, y_heads[0], y_heads[1])
            hp = hstate[cols, :]
            y_inter = _dot_nt(cm, hp) * eacum_x[:, cols]
            sdec = jnp.where(first_row, sdec_t[hd0:hd0 + 1, :], sdec_t[hd0 + 1:hd0 + 2, :])
            hstate[cols, :] = hp * sdec + _dot_tn(xp * wl_x[:, cols], bm)
            ybuf[:, cols] = y_intra + y_inter + xp * dskip_ref[:, cols]

    y = ybuf[...] * jax.nn.silu(z_ref[...])
    y = y * lax.rsqrt(jnp.mean(y * y, axis=-1, keepdims=True) + EPS) * gain_ref[...]
    y_ref[...] = y.astype(y_ref.dtype)

    @pl.when(c == n_chunks - 1)
    def _():
        h_out_ref[0] = hstate[...]


def ssd(sxbc, sz, small, small_bias, a_log, d_skip, conv_w, conv_b, gain, conv0, h0,
        *, n_chunks, n, n_valid, out_dtype):
    batch = h0.shape[0]
    rows = SSD_HEADS * SSD_HEADDIM
    alog = jnp.pad(a_log.astype(F32), (0, LANES - SSD_HEADS)).reshape(1, LANES)
    dskip = jnp.repeat(d_skip.astype(F32), SSD_HEADDIM).reshape(1, SSD_INNER)
    expand = (jnp.arange(LANES)[:, None] == (jnp.arange(SSD_INNER) // SSD_HEADDIM)[None, :]).astype(F32)
    conv0 = jnp.pad(conv0.astype(F32), ((0, 0), (CONV_ROWS - (SSD_CONV - 1), 0), (0, 0)))

    def tok(width):
        return pl.BlockSpec((n, width), lambda b, c: (b * n_chunks + c, 0))

    def const(shape):
        return pl.BlockSpec(shape, lambda b, c: (0,) * len(shape))

    y, h1 = pl.pallas_call(
        functools.partial(_ssd_kernel, n_valid=n_valid, n_chunks=n_chunks),
        grid=(batch, n_chunks),
        in_specs=[tok(SSD_XBC), tok(SSD_INNER), tok(LANES), const((1, LANES)), const((1, LANES)),
                  const((1, SSD_INNER)), const((SSD_CONV, SSD_XBC)), const((1, SSD_XBC)),
                  const((1, SSD_INNER)), const((LANES, SSD_INNER)),
                  pl.BlockSpec((1, CONV_ROWS, SSD_XBC), lambda b, c: (b, 0, 0)),
                  pl.BlockSpec((1, rows, SSD_STATE), lambda b, c: (b, 0, 0))],
        out_specs=[tok(SSD_INNER), pl.BlockSpec((1, rows, SSD_STATE), lambda b, c: (b, 0, 0))],
        out_shape=[jax.ShapeDtypeStruct((batch * n_chunks * n, SSD_INNER), out_dtype),
                   jax.ShapeDtypeStruct((batch, rows, SSD_STATE), F32)],
        scratch_shapes=[pltpu.VMEM((rows, SSD_STATE), F32), pltpu.VMEM((CONV_ROWS, SSD_XBC), F32),
                        pltpu.VMEM((n, SSD_XBC), F32), pltpu.VMEM((n, SSD_INNER), F32)],
        compiler_params=_params("parallel", "arbitrary"),
        name="ssd",
    )(sxbc, sz, small, small_bias, alog, dskip, conv_w.astype(F32), conv_b.reshape(1, SSD_XBC).astype(F32),
      gain.reshape(1, SSD_INNER).astype(F32), expand, conv0, h0.reshape(batch, rows, SSD_STATE))
    return y, h1.reshape(batch, SSD_HEADS, SSD_HEADDIM, SSD_STATE)


SAMPLE_ROWS = 8


def _rope_tables(pos):
    inv = ROPE_BASE ** (-jnp.arange(0, RET_DK, 2, dtype=F32) / RET_DK)
    ang = pos[:, None] * inv[None, :]
    return jnp.cos(ang), jnp.sin(ang)


def _col_slice(w, lo, hi):
    return w[:, IN_OFFSETS[lo]:IN_OFFSETS[hi]].astype(BF16)


def kernel(x_prompt, x_sample, state_ret, state_ssm, state_conv, state_mlstm_c, state_mlstm_n, state_mlstm_m, p_prompt, p_sample, norm_mix, w_in, ret_norm_gain, ssd_conv_w, ssd_conv_b, ssd_dt_bias, ssd_a_log, ssd_d, ssd_norm_gain, mlstm_b_i, mlstm_b_f, mlstm_norm_gain, w_branch_ret, w_branch_ssd, w_branch_mlstm, w_out, norm_ffn, peer_w_q, peer_keys1, peer_keys2, peer_u, peer_v, norm_ple, w_ple_gate, w_ple, norm_final):
    bp, tp, d = x_prompt.shape
    bs, ts, _ = x_sample.shape
    n_p = bp * tp
    n_s = bs * ts
    assert tp % CHUNK == 0 and SSD_CONV - 1 <= ts <= SAMPLE_ROWS
    h = jnp.concatenate([x_prompt.reshape(n_p, d), x_sample.reshape(n_s, d)], axis=0)

    cos_p, sin_p = _rope_tables(jnp.arange(tp, dtype=F32))
    cos_s, sin_s = _rope_tables(PAST_LEN + jnp.arange(SAMPLE_ROWS, dtype=F32))
    prompt_cfg = dict(n_chunks=tp // CHUNK, n=CHUNK, n_valid=CHUNK, out_dtype=BF16)
    sample_cfg = dict(n_chunks=1, n=SAMPLE_ROWS, n_valid=ts, out_dtype=F32)

    def sample_rows(a):
        a = a[n_p:].reshape(bs, ts, a.shape[-1])
        return jnp.pad(a, ((0, 0), (0, SAMPLE_ROWS - ts), (0, 0))).reshape(bs * SAMPLE_ROWS, a.shape[-1])

    def join(y_p, y_s):
        y_s = y_s.reshape(bs, SAMPLE_ROWS, -1)[:, :ts].reshape(n_s, -1).astype(BF16)
        return jnp.concatenate([y_p, y_s], axis=0)

    new_states = [[[] for _ in range(6)] for _ in range(2)]
    for l in range(DEPTH):
        u = rmsnorm([h], norm_mix[l], BF16)
        w = w_in[l]
        seg = [matmul(u, _col_slice(w, i, i + 1), name="in_proj")
               if IN_SIZES[i] % LANES == 0 else None for i in range(len(IN_SIZES))]
        small_w = jnp.concatenate([w[:, IN_OFFSETS[6]:IN_OFFSETS[7]], w[:, IN_OFFSETS[11]:IN_OFFSETS[13]]],
                                  axis=1)
        n_small = small_w.shape[1]
        small_w = jnp.pad(small_w, ((0, 0), (0, LANES - n_small))).astype(BF16)
        small = matmul(u, small_w, name="in_proj_small")
        small_bias = jnp.pad(jnp.concatenate([ssd_dt_bias[l], mlstm_b_i[l], mlstm_b_f[l]]).astype(F32),
                             (0, LANES - n_small)).reshape(1, LANES)
        rq, rk, rv, rg, sz, sxbc = seg[0:6]
        mq, mk, mv, mo = seg[7:11]
        gates = seg[13]

        yr_p, ret_p = retention(rq, rk, rv, rg, cos_p, sin_p, ret_norm_gain[l],
                                jnp.zeros((bp, RET_HEADS, RET_DK, RET_DV), F32), **prompt_cfg)
        ys_p, ssm_p = ssd(sxbc, sz, small, small_bias, ssd_a_log[l], ssd_d[l], ssd_conv_w[l], ssd_conv_b[l],
                          ssd_norm_gain[l], jnp.zeros((bp, SSD_CONV - 1, SSD_XBC), F32),
                          jnp.zeros((bp, SSD_HEADS, SSD_HEADDIM, SSD_STATE), F32), **prompt_cfg)
        ym_p, c_p, nn_p, m_p = mlstm(mq, mk, mv, mo, small, small_bias, mlstm_norm_gain[l],
                                     jnp.zeros((bp, MLSTM_HEADS, MLSTM_DK, MLSTM_DV), F32),
                                     jnp.zeros((bp, MLSTM_HEADS, MLSTM_DK), F32),
                                     jnp.full((bp, MLSTM_HEADS), -jnp.inf, F32), **prompt_cfg)
        conv_p = sxbc[:n_p].reshape(bp, tp, SSD_XBC)[:, tp - (SSD_CONV - 1):]

        small_s = sample_rows(small)
        yr_s, ret_s = retention(sample_rows(rq), sample_rows(rk), sample_rows(rv), sample_rows(rg),
                                cos_s, sin_s, ret_norm_gain[l], state_ret[l], **sample_cfg)
        ys_s, ssm_s = ssd(sample_rows(sxbc), sample_rows(sz), small_s, small_bias, ssd_a_log[l], ssd_d[l],
                          ssd_conv_w[l], ssd_conv_b[l], ssd_norm_gain[l], state_conv[l], state_ssm[l],
                          **sample_cfg)
        ym_s, c_s, nn_s, m_s = mlstm(sample_rows(mq), sample_rows(mk), sample_rows(mv), sample_rows(mo),
                                     small_s, small_bias, mlstm_norm_gain[l], state_mlstm_c[l],
                                     state_mlstm_n[l], state_mlstm_m[l], **sample_cfg)
        conv_s = sxbc[n_p:].reshape(bs, ts, SSD_XBC)[:, ts - (SSD_CONV - 1):]

        for slot, val in zip(new_states[0], (ret_p, ssm_p, conv_p, c_p, nn_p, m_p)):
            slot.append(val)
        for slot, val in zip(new_states[1], (ret_s, ssm_s, conv_s, c_s, nn_s, m_s)):
            slot.append(val)
        y_r, y_s, y_m = join(yr_p, yr_s), join(ys_p, ys_s), join(ym_p, ym_s)

        gate_cols = D_MODEL // 1024
        merged = matmul(y_r, w_branch_ret[l].astype(BF16), epilogue=_gated_epilogue,
                        extras=((gates, 0),), name="branch_ret")
        merged = matmul(y_s, w_branch_ssd[l].astype(BF16), epilogue=_gated_add_epilogue,
                        extras=((gates, gate_cols), (merged, 0)), name="branch_ssd")
        merged = matmul(y_m, w_branch_mlstm[l].astype(BF16), out_dtype=BF16, epilogue=_gated_add_epilogue,
                        extras=((gates, 2 * gate_cols), (merged, 0)), name="branch_mlstm")
        h = matmul(merged, w_out[l].astype(BF16), epilogue=_residual_epilogue, extras=((h, 0),),
                   name="w_out")

        uf = rmsnorm([h], norm_ffn[l], BF16)
        q = matmul(uf, peer_w_q[l].astype(BF16), name="peer_q")
        s1, s2, e1, e2, thr = peer_scores(q, peer_keys1[l], peer_keys2[l])
        peer_out = peer_dense(uf, peer_u[l].astype(BF16), peer_v[l].astype(BF16), s1, s2, e1, e2, thr)

        up, h = rmsnorm([h, peer_out], norm_ple[l], BF16, emit_sum=True)
        pe = jnp.concatenate([p_prompt[l].reshape(n_p, -1), p_sample[l].reshape(n_s, -1)], axis=0)
        ple = matmul(pe.astype(BF16), w_ple[l].astype(BF16), name="ple")
        h = matmul(up, w_ple_gate[l].astype(BF16), epilogue=_ple_epilogue, extras=((h, 0), (ple, 0)),
                   name="ple_gate")

    y = rmsnorm([h], norm_final, F32)
    outs = [y[:n_p].reshape(bp, tp, d), y[n_p:].reshape(bs, ts, d)]
    for group in new_states:
        outs.extend(jnp.stack(slot) for slot in group)
    return tuple(outs)
```

```python
import functools

import numpy as np
import jax
import jax.numpy as jnp
from jax import lax
from jax.experimental import pallas as pl
from jax.experimental.pallas import tpu as pltpu

D_MODEL = 4096
DEPTH = 2
PAST_LEN = 16384

RET_HEADS = 8
RET_DK = D_MODEL // 16
RET_DV = D_MODEL // 8
RET_QK = RET_HEADS * RET_DK
RET_V = RET_HEADS * RET_DV
SSD_HEADDIM = 64
SSD_HEADS = D_MODEL // SSD_HEADDIM
SSD_INNER = SSD_HEADS * SSD_HEADDIM
SSD_STATE = 128
SSD_GROUPS = 8
SSD_HPG = SSD_HEADS // SSD_GROUPS
SSD_CONV = 4
SSD_XBC = SSD_INNER + 2 * SSD_GROUPS * SSD_STATE
MLSTM_HEADS = 8
MLSTM_DK = D_MODEL // 16
MLSTM_DV = D_MODEL // 8
MLSTM_QK = MLSTM_HEADS * MLSTM_DK
MLSTM_V = MLSTM_HEADS * MLSTM_DV
PEER_HEADS = 8
PEER_NKEYS = 128
PEER_DKEY = 256
PEER_TOPK = 16
CHUNK = 128
ROPE_BASE = 10000.0
EPS = 1e-6

IN_SIZES = (RET_QK, RET_QK, RET_V, RET_V,
            SSD_INNER, SSD_XBC, SSD_HEADS,
            MLSTM_QK, MLSTM_QK, MLSTM_V, MLSTM_V, MLSTM_HEADS, MLSTM_HEADS,
            3 * D_MODEL)
IN_OFFSETS = tuple(int(v) for v in np.cumsum((0,) + IN_SIZES))

F32 = jnp.float32
BF16 = jnp.bfloat16

V7X_VMEM_BYTES = 64 * 1024 * 1024
VMEM_LIMIT = V7X_VMEM_BYTES - 8 * 1024 * 1024
LANES = 128


def _params(*semantics):
    return pltpu.CompilerParams(dimension_semantics=semantics, vmem_limit_bytes=VMEM_LIMIT)


def _rmsnorm_kernel(*refs, n_in, emit_sum):
    xs = refs[:n_in]
    g_ref = refs[n_in]
    outs = refs[n_in + 1:]
    x = xs[0][...]
    for r in xs[1:]:
        x = x + r[...]
    y = x * lax.rsqrt(jnp.mean(x * x, axis=-1, keepdims=True) + EPS) * g_ref[...]
    outs[0][...] = y.astype(outs[0].dtype)
    if emit_sum:
        outs[1][...] = x


def rmsnorm(xs, gain, out_dtype, emit_sum=False, tm=256):
    t, d = xs[0].shape
    row = pl.BlockSpec((tm, d), lambda i: (i, 0))
    out_shape = [jax.ShapeDtypeStruct((t, d), out_dtype)]
    out_specs = [row]
    if emit_sum:
        out_shape.append(jax.ShapeDtypeStruct((t, d), F32))
        out_specs.append(row)
    res = pl.pallas_call(
        functools.partial(_rmsnorm_kernel, n_in=len(xs), emit_sum=emit_sum),
        grid=(t // tm,),
        in_specs=[row] * len(xs) + [pl.BlockSpec((1, d), lambda i: (0, 0))],
        out_specs=out_specs,
        out_shape=out_shape,
        compiler_params=_params("parallel"),
        name="rmsnorm",
    )(*xs, gain.reshape(1, d).astype(F32))
    return res if emit_sum else res[0]


def _matmul_kernel(a_ref, b_ref, *refs, epilogue):
    *extra, o_ref = refs
    acc = jnp.dot(a_ref[...], b_ref[...], preferred_element_type=F32)
    if epilogue is not None:
        acc = epilogue(acc, *[e[...] for e in extra])
    o_ref[...] = acc.astype(o_ref.dtype)


def matmul(a, b, out_dtype=F32, epilogue=None, extras=(), tm=512, tn=1024, name="matmul"):
    m, k = a.shape
    _, n = b.shape
    tn = min(tn, n)
    tm = min(tm, m)
    in_specs = [pl.BlockSpec((tm, k), lambda j, i: (i, 0)),
                pl.BlockSpec((k, tn), lambda j, i: (0, j))]
    for _, off in extras:
        in_specs.append(pl.BlockSpec((tm, tn), lambda j, i, off=off: (i, j + off)))
    return pl.pallas_call(
        functools.partial(_matmul_kernel, epilogue=epilogue),
        grid=(n // tn, m // tm),
        in_specs=in_specs,
        out_specs=pl.BlockSpec((tm, tn), lambda j, i: (i, j)),
        out_shape=jax.ShapeDtypeStruct((m, n), out_dtype),
        compiler_params=_params("parallel", "parallel"),
        name=name,
    )(a, b, *[e for e, _ in extras])


def _gated_epilogue(acc, gate):
    return jax.nn.sigmoid(gate) * acc


def _gated_add_epilogue(acc, gate, prev):
    return prev + jax.nn.sigmoid(gate) * acc


def _residual_epilogue(acc, res):
    return res + acc


def _ple_epilogue(acc, h, ple):
    return h + jax.nn.sigmoid(acc) * ple


def _top_values(s, k):
    n = s.shape[0]
    row = lax.broadcasted_iota(jnp.int32, s.shape, 0)
    vals = []
    for _ in range(k):
        m = jnp.max(s, axis=0, keepdims=True)
        first = jnp.min(jnp.where(s == m, row, n), axis=0, keepdims=True)
        s = jnp.where(row == first, -jnp.inf, s)
        vals.append(m)
    return vals


def _peer_scores_kernel(q_ref, k1_ref, k2_ref, s1_ref, s2_ref, e1_ref, e2_ref, thr_ref):
    half = PEER_DKEY // 2
    contract_last = (((1,), (1,)), ((), ()))
    thr_rows = []
    for h in range(PEER_HEADS):
        q1 = q_ref[:, h * PEER_DKEY: h * PEER_DKEY + half]
        q2 = q_ref[:, h * PEER_DKEY + half: (h + 1) * PEER_DKEY]
        s1 = lax.dot_general(k1_ref[...], q1, contract_last, precision=lax.Precision.HIGHEST,
                             preferred_element_type=F32)
        s2 = lax.dot_general(k2_ref[...], q2, contract_last, precision=lax.Precision.HIGHEST,
                             preferred_element_type=F32)
        v1 = _top_values(s1, PEER_TOPK)
        v2 = _top_values(s2, PEER_TOPK)
        v1_all = jnp.concatenate(v1, axis=0)
        v2_all = jnp.concatenate(v2, axis=0)
        sub = PEER_TOPK // 2
        cand = jnp.concatenate([v1[0] + v2_all] + [v1[a] + v2_all[:sub] for a in range(1, sub)]
                               + [v1_all[sub:] + v2[0]], axis=0)
        sc = _top_values(cand, PEER_TOPK)
        top = sc[0]
        z = jnp.exp(sc[0] - top)
        for c in sc[1:]:
            z = z + jnp.exp(c - top)
        s1_ref[h] = s1
        s2_ref[h] = s2
        e1_ref[h] = jnp.exp(s1 - v1[0])
        e2_ref[h] = jnp.exp(s2 - v2[0]) / z
        thr_rows.append(sc[PEER_TOPK - 1])
    thr_ref[...] = jnp.concatenate(thr_rows, axis=0)


def peer_scores(q, keys1, keys2, tb=256):
    t = q.shape[0]
    big = jax.ShapeDtypeStruct((PEER_HEADS, PEER_NKEYS, t), F32)
    big_spec = pl.BlockSpec((PEER_HEADS, PEER_NKEYS, tb), lambda i: (0, 0, i))
    key_spec = pl.BlockSpec((PEER_NKEYS, PEER_DKEY // 2), lambda i: (0, 0))
    return pl.pallas_call(
        _peer_scores_kernel,
        grid=(t // tb,),
        in_specs=[pl.BlockSpec((tb, PEER_HEADS * PEER_DKEY), lambda i: (i, 0)), key_spec, key_spec],
        out_specs=[big_spec] * 4 + [pl.BlockSpec((PEER_HEADS, tb), lambda i: (0, i))],
        out_shape=[big] * 4 + [jax.ShapeDtypeStruct((PEER_HEADS, t), F32)],
        compiler_params=_params("parallel"),
        name="peer_scores",
    )(q, keys1.astype(F32), keys2.astype(F32))


def _peer_dense_kernel(x_ref, u_ref, v_ref, s1_ref, s2_ref, e1_ref, e2_ref, thr_ref, o_ref, *, rows):
    j = pl.program_id(1)

    @pl.when(j == 0)
    def _():
        o_ref[...] = jnp.zeros_like(o_ref)

    contract_last = (((1,), (1,)), ((), ()))
    hid = lax.dot_general(u_ref[...], x_ref[...], contract_last, preferred_element_type=F32)
    act = 0.5 * hid * (1.0 + lax.erf(hid * (0.5 ** 0.5)))
    pieces = []
    for r in range(rows):
        i = j * rows + r
        gate = None
        for h in range(PEER_HEADS):
            s1 = s1_ref[h, pl.ds(i, 1), :]
            e1 = e1_ref[h, pl.ds(i, 1), :]
            keep = (s1 + s2_ref[h]) >= thr_ref[pl.ds(h, 1), :]
            term = jnp.where(keep, e1 * e2_ref[h], 0.0)
            gate = term if gate is None else gate + term
        pieces.append(gate * act[r * PEER_NKEYS:(r + 1) * PEER_NKEYS])
    weighted = jnp.concatenate(pieces, axis=0) if rows > 1 else pieces[0]
    o_ref[...] += jnp.dot(weighted.T.astype(BF16), v_ref[...], preferred_element_type=F32)


def peer_dense(x, u, v, s1, s2, e1, e2, thr, tb=512, eb=256):
    t, d = x.shape
    n_exp = u.shape[0]
    rows = eb // PEER_NKEYS
    tok_spec = pl.BlockSpec((PEER_HEADS, PEER_NKEYS, tb), lambda i, j: (0, 0, i),
                            pipeline_mode=pl.Buffered(1))
    return pl.pallas_call(
        functools.partial(_peer_dense_kernel, rows=rows),
        grid=(t // tb, n_exp // eb),
        in_specs=[pl.BlockSpec((tb, d), lambda i, j: (i, 0), pipeline_mode=pl.Buffered(1)),
                  pl.BlockSpec((eb, d), lambda i, j: (j, 0)),
                  pl.BlockSpec((eb, d), lambda i, j: (j, 0)),
                  tok_spec, tok_spec, tok_spec, tok_spec,
                  pl.BlockSpec((PEER_HEADS, tb), lambda i, j: (0, i), pipeline_mode=pl.Buffered(1))],
        out_specs=pl.BlockSpec((tb, d), lambda i, j: (i, 0)),
        out_shape=jax.ShapeDtypeStruct((t, d), F32),
        compiler_params=_params("parallel", "arbitrary"),
        name="peer_dense",
    )(x, u, v, s1, s2, e1, e2, thr)


NT_DIMS = (((1,), (1,)), ((), ()))
TN_DIMS = (((0,), (0,)), ((), ()))


def _dot(a, b):
    return jnp.dot(a.astype(BF16), b.astype(BF16), preferred_element_type=F32)


def _dot_nt(a, b):
    return lax.dot_general(a.astype(BF16), b.astype(BF16), NT_DIMS, preferred_element_type=F32)


def _dot_tn(a, b):
    return lax.dot_general(a.astype(BF16), b.astype(BF16), TN_DIMS, preferred_element_type=F32)


def _rotate(x, cos, sin):
    half = x.shape[1] // 2
    x1, x2 = x[:, :half], x[:, half:]
    return jnp.concatenate([x1 * cos - x2 * sin, x1 * sin + x2 * cos], axis=1)


def _head_norm(o, gain):
    c = o - jnp.mean(o, axis=-1, keepdims=True)
    return c * lax.rsqrt(jnp.mean(c * c, axis=-1, keepdims=True) + EPS) * gain


def _softplus(x):
    return jnp.maximum(x, 0.0) + jnp.log1p(jnp.exp(-jnp.abs(x)))


def _log_sigmoid(x):
    return jnp.minimum(x, 0.0) - jnp.log1p(jnp.exp(-jnp.abs(x)))


def _causal(n):
    t = lax.broadcasted_iota(jnp.int32, (n, n), 0)
    s = lax.broadcasted_iota(jnp.int32, (n, n), 1)
    return t, s, t >= s


def _write_state(out_ref, prev_refs, value, lead=(0, 0)):
    if not prev_refs:
        out_ref[lead] = value
        return
    for idx, p in enumerate(prev_refs):
        out_ref[(idx,) + lead] = p[lead]
    out_ref[(len(prev_refs),) + lead] = value


def _state_specs(s0, layer, prevs, one, index):
    in_spec = pl.BlockSpec((None,) + one, lambda *g: (layer,) + index(*g))
    prev_specs = [pl.BlockSpec(one, index) for _ in prevs]
    if prevs:
        depth = len(prevs) + 1
        out_spec = pl.BlockSpec((depth,) + one, lambda *g: (0,) + index(*g))
        out_shape = jax.ShapeDtypeStruct((depth,) + s0.shape[1:], F32)
    else:
        out_spec = pl.BlockSpec(one, index)
        out_shape = jax.ShapeDtypeStruct(s0.shape[1:], F32)
    return in_spec, prev_specs, out_spec, out_shape


def _retention_kernel(q_ref, k_ref, v_ref, g_ref, cos_ref, sin_ref, lg_ref, gain_ref, s0_ref,
                      *refs, n_valid, n_chunks):
    *prev_refs, y_ref, s_out_ref, state = refs
    c = pl.program_id(1)
    n = q_ref.shape[0]

    @pl.when(c == 0)
    def _():
        state[...] = s0_ref[0]

    cos, sin = cos_ref[...], sin_ref[...]
    t, s, causal = _causal(n)
    dist = jnp.where(causal, t - s, 0).astype(F32)
    pos = lax.broadcasted_iota(jnp.int32, (n, 1), 0)
    posf = pos.astype(F32)
    for h in range(RET_HEADS):
        qk = slice(h * RET_DK, (h + 1) * RET_DK)
        vv = slice(h * RET_DV, (h + 1) * RET_DV)
        lg = lg_ref[h][:, :1]
        q = _rotate(q_ref[:, qk], cos, sin)
        k = _rotate(k_ref[:, qk], cos, sin) * (RET_DK ** -0.5)
        v = v_ref[:, vv]
        decay = jnp.where(causal, jnp.exp(dist * lg), 0.0)
        prev = state[h]
        o = _dot(_dot_nt(q, k) * decay, v) + _dot(q, prev) * jnp.exp((posf + 1.0) * lg)
        w_to = jnp.where(pos < n_valid, jnp.exp((n_valid - 1.0 - posf) * lg), 0.0)
        state[h] = prev * jnp.exp(n_valid * lg) + _dot_tn(k * w_to, v)
        y = jax.nn.silu(g_ref[:, vv]) * _head_norm(o, gain_ref[:, vv])
        y_ref[:, vv] = y.astype(y_ref.dtype)

    @pl.when(c == n_chunks - 1)
    def _():
        _write_state(s_out_ref, prev_refs, state[...], lead=(0,))


def retention(rq, rk, rv, rg, cos, sin, gain, s0, layer, prevs, *, n_chunks, n, n_valid, out_dtype):
    batch = s0.shape[1]
    log_gamma = jnp.log1p(-jnp.exp2(-5.0 - jnp.arange(RET_HEADS, dtype=F32)))
    lg = jnp.broadcast_to(log_gamma[:, None, None], (RET_HEADS, 1, LANES))

    def tok(width):
        return pl.BlockSpec((n, width), lambda b, c: (b * n_chunks + c, 0))

    def const(shape):
        return pl.BlockSpec(shape, lambda b, c: (0,) * len(shape))

    rope = pl.BlockSpec((n, RET_DK // 2), lambda b, c: (c, 0))
    st_in, st_prev, st_out, st_shape = _state_specs(s0, layer, prevs, (1, RET_HEADS, RET_DK, RET_DV),
                                                    lambda b, c: (b, 0, 0, 0))
    return pl.pallas_call(
        functools.partial(_retention_kernel, n_valid=n_valid, n_chunks=n_chunks),
        grid=(batch, n_chunks),
        in_specs=[tok(RET_QK), tok(RET_QK), tok(RET_V), tok(RET_V), rope, rope,
                  const((RET_HEADS, 1, LANES)), const((1, RET_V)), st_in] + st_prev,
        out_specs=[tok(RET_V), st_out],
        out_shape=[jax.ShapeDtypeStruct((batch * n_chunks * n, RET_V), out_dtype), st_shape],
        scratch_shapes=[pltpu.VMEM((RET_HEADS, RET_DK, RET_DV), F32)],
        compiler_params=_params("parallel", "arbitrary"),
        name="retention",
    )(rq, rk, rv, rg, cos, sin, lg, gain.reshape(1, RET_V).astype(F32), s0, *prevs)


def _mlstm_kernel(q_ref, k_ref, v_ref, og_ref, small_ref, bias_ref, gain_ref, n0_ref, m0_ref, c0_ref,
                  *refs, n_valid, n_chunks):
    *prev_refs, y_ref, n_out_ref, m_out_ref, c_out_ref, c_s, n_s, m_s = refs
    c = pl.program_id(1)
    n = q_ref.shape[0]

    @pl.when(c == 0)
    def _():
        c_s[...] = c0_ref[0]
        n_s[...] = n0_ref[0]
        m_s[...] = m0_ref[0]

    sm = small_ref[...] + bias_ref[...]
    sm_t = sm.T
    lane = lax.broadcasted_iota(jnp.int32, sm.shape, 1)
    sub = lax.broadcasted_iota(jnp.int32, sm_t.shape, 0)

    def col(idx):
        return jnp.sum(jnp.where(lane == idx, sm, 0.0), axis=1, keepdims=True)

    def row(idx):
        return jnp.sum(jnp.where(sub == idx, sm_t, 0.0), axis=0, keepdims=True)

    t, s, causal = _causal(n)
    pos = lax.broadcasted_iota(jnp.int32, (n, 1), 0)
    for h in range(MLSTM_HEADS):
        qk = slice(h * MLSTM_DK, (h + 1) * MLSTM_DK)
        vv = slice(h * MLSTM_DV, (h + 1) * MLSTM_DV)
        i_idx = SSD_HEADS + h
        f_idx = SSD_HEADS + MLSTM_HEADS + h
        ig_col, ig_row = col(i_idx), row(i_idx)
        lf_col, lf_row = _log_sigmoid(col(f_idx)), _log_sigmoid(row(f_idx))
        fcum_col = jnp.sum(jnp.where(causal, lf_row, 0.0), axis=1, keepdims=True)
        fcum_row = jnp.sum(jnp.where(t <= s, lf_col, 0.0), axis=0, keepdims=True)
        dlog = jnp.where(causal, fcum_col - fcum_row + ig_row, -jnp.inf)
        m0 = m_s[h][:, :1]
        b_log = fcum_col + m0
        m_t = jnp.maximum(b_log, jnp.max(dlog, axis=1, keepdims=True))
        dm = jnp.exp(dlog - m_t)
        inter = jnp.exp(b_log - m_t)

        q = q_ref[:, qk]
        k = k_ref[:, qk] * (MLSTM_DK ** -0.5)
        v = v_ref[:, vv]
        sc = _dot_nt(q, k) * dm
        c_prev = c_s[h]
        n_prev = n_s[h]
        num = _dot(sc, v) + inter * _dot(q, c_prev)
        den = jnp.sum(sc, axis=1, keepdims=True) + inter * jnp.sum(q * n_prev, axis=1, keepdims=True)
        hval = num / jnp.maximum(jnp.abs(den), jnp.exp(-m_t))

        m_new = m_t[n_valid - 1:n_valid, :]
        f_last = fcum_col[n_valid - 1:n_valid, :]
        wk = jnp.where(pos < n_valid, jnp.exp(f_last - fcum_col + ig_col - m_new), 0.0)
        carry = jnp.exp(f_last + m0 - m_new)
        kw = k * wk
        c_s[h] = c_prev * carry + _dot_tn(kw, v)
        n_s[h] = n_prev * carry + jnp.sum(kw, axis=0, keepdims=True)
        m_s[h] = jnp.broadcast_to(m_new, (1, LANES))

        y = jax.nn.sigmoid(og_ref[:, vv]) * _head_norm(hval, gain_ref[:, vv])
        y_ref[:, vv] = y.astype(y_ref.dtype)

    @pl.when(c == n_chunks - 1)
    def _():
        _write_state(c_out_ref, prev_refs, c_s[...], lead=(0,))
        n_out_ref[0] = n_s[...]
        m_out_ref[0] = m_s[...]


def mlstm(mq, mk, mv, mo, small, small_bias, gain, c0, layer, prevs, n0, m0,
          *, n_chunks, n, n_valid, out_dtype):
    batch = c0.shape[1]
    n0 = n0.reshape(batch, MLSTM_HEADS, 1, MLSTM_DK)
    m0 = jnp.broadcast_to(m0[:, :, None, None], (batch, MLSTM_HEADS, 1, LANES))

    def tok(width):
        return pl.BlockSpec((n, width), lambda b, c: (b * n_chunks + c, 0))

    def const(shape):
        return pl.BlockSpec(shape, lambda b, c: (0,) * len(shape))

    def st(*tail):
        return pl.BlockSpec((1, MLSTM_HEADS) + tail, lambda b, c: (b, 0, 0, 0))

    c_in, c_prev, c_out, c_shape = _state_specs(c0, layer, prevs, (1, MLSTM_HEADS, MLSTM_DK, MLSTM_DV),
                                                lambda b, c: (b, 0, 0, 0))
    y, n1, m1, c1 = pl.pallas_call(
        functools.partial(_mlstm_kernel, n_valid=n_valid, n_chunks=n_chunks),
        grid=(batch, n_chunks),
        in_specs=[tok(MLSTM_QK), tok(MLSTM_QK), tok(MLSTM_V), tok(MLSTM_V), tok(LANES),
                  const((1, LANES)), const((1, MLSTM_V)), st(1, MLSTM_DK), st(1, LANES), c_in] + c_prev,
        out_specs=[tok(MLSTM_V), st(1, MLSTM_DK), st(1, LANES), c_out],
        out_shape=[jax.ShapeDtypeStruct((batch * n_chunks * n, MLSTM_V), out_dtype),
                   jax.ShapeDtypeStruct(n0.shape, F32),
                   jax.ShapeDtypeStruct(m0.shape, F32), c_shape],
        scratch_shapes=[pltpu.VMEM((MLSTM_HEADS, MLSTM_DK, MLSTM_DV), F32),
                        pltpu.VMEM((MLSTM_HEADS, 1, MLSTM_DK), F32),
                        pltpu.VMEM((MLSTM_HEADS, 1, LANES), F32)],
        compiler_params=_params("parallel", "arbitrary"),
        name="mlstm",
    )(mq, mk, mv, mo, small, small_bias, gain.reshape(1, MLSTM_V).astype(F32), n0, m0, c0, *prevs)
    return y, c1, n1.reshape(batch, MLSTM_HEADS, MLSTM_DK), m1[:, :, 0, 0]


SSD_BC = SSD_GROUPS * SSD_STATE
CONV_ROWS = 8


def _ssd_kernel(xbc_ref, z_ref, small_ref, bias_ref, alog_ref, dskip_ref, cw_ref, cb_ref, gain_ref,
                expand_ref, conv0_ref, h0_ref, *refs, n_valid, n_chunks):
    *prev_refs, y_ref, h_out_ref, hstate, xprev, xc_s, ybuf = refs
    c = pl.program_id(1)
    n = xbc_ref.shape[0]
    pair = 2 * SSD_HEADDIM

    @pl.when(c == 0)
    def _():
        hstate[...] = h0_ref[0]
        xprev[...] = conv0_ref[0]

    cblk = 512
    for j in range(SSD_XBC // cblk):
        cols = slice(j * cblk, (j + 1) * cblk)
        raw = xbc_ref[:, cols]
        ext = jnp.concatenate([xprev[:, cols], raw], axis=0)
        acc = cb_ref[:, cols] + cw_ref[SSD_CONV - 1:SSD_CONV, cols] * raw
        for i in range(SSD_CONV - 1):
            lo = CONV_ROWS - (SSD_CONV - 1) + i
            acc = acc + cw_ref[i:i + 1, cols] * ext[lo:lo + n]
        xc_s[:, cols] = acc * jax.nn.sigmoid(acc)
        xprev[:, cols] = raw[n - CONV_ROWS:n]

    head_lane = lax.broadcasted_iota(jnp.int32, (1, LANES), 1) < SSD_HEADS
    dt = jnp.where(head_lane, _softplus(small_ref[...] + bias_ref[...]), 0.0)
    a = dt * jnp.where(head_lane, -jnp.exp(alog_ref[...]), 0.0)
    t, s, causal = _causal(n)
    acum = jnp.dot(causal.astype(F32), a, precision=lax.Precision.HIGHEST,
                   preferred_element_type=F32)
    pos = lax.broadcasted_iota(jnp.int32, (n, 1), 0)
    a_last = acum[n_valid - 1:n_valid, :]
    wl = jnp.where(pos < n_valid, jnp.exp(a_last - acum) * dt, 0.0)
    acum_t = acum.T
    dt_t = dt.T
    sdec_t = jnp.exp(acum_t[:, n_valid - 1:n_valid])
    expand = expand_ref[...]
    eacum_x = jnp.dot(jnp.exp(acum), expand, precision=lax.Precision.HIGHEST, preferred_element_type=F32)
    wl_x = jnp.dot(wl, expand, precision=lax.Precision.HIGHEST, preferred_element_type=F32)

    first_lane = lax.broadcasted_iota(jnp.int32, (n, pair), 1) < SSD_HEADDIM
    first_row = lax.broadcasted_iota(jnp.int32, (pair, 1), 0) < SSD_HEADDIM
    for g in range(SSD_GROUPS):
        bm = xc_s[:, SSD_INNER + g * SSD_STATE:SSD_INNER + (g + 1) * SSD_STATE]
        cm = xc_s[:, SSD_INNER + SSD_BC + g * SSD_STATE:SSD_INNER + SSD_BC + (g + 1) * SSD_STATE]
        cb = _dot_nt(cm, bm)
        for pr in range(SSD_HPG // 2):
            hd0 = g * SSD_HPG + 2 * pr
            cols = slice(hd0 * SSD_HEADDIM, hd0 * SSD_HEADDIM + pair)
            xp = xc_s[:, cols]
            y_heads = []
            for hd in (hd0, hd0 + 1):
                seg = acum[:, hd:hd + 1] - acum_t[hd:hd + 1, :]
                w = jnp.where(causal, jnp.exp(jnp.where(causal, seg, 0.0)), 0.0) * cb * dt_t[hd:hd + 1, :]
                y_heads.append(_dot(w, xp))
            y_intra = jnp.where(first_lane, y_heads[0], y_heads[1])
            hp = hstate[cols, :]
            y_inter = _dot_nt(cm, hp) * eacum_x[:, cols]
            sdec = jnp.where(first_row, sdec_t[hd0:hd0 + 1, :], sdec_t[hd0 + 1:hd0 + 2, :])
            hstate[cols, :] = hp * sdec + _dot_tn(xp * wl_x[:, cols], bm)
            ybuf[:, cols] = y_intra + y_inter + xp * dskip_ref[:, cols]

    y = ybuf[...] * jax.nn.silu(z_ref[...])
    y = y * lax.rsqrt(jnp.mean(y * y, axis=-1, keepdims=True) + EPS) * gain_ref[...]
    y_ref[...] = y.astype(y_ref.dtype)

    @pl.when(c == n_chunks - 1)
    def _():
        _write_state(h_out_ref, prev_refs, hstate[...], lead=(0,))


def ssd(sxbc, sz, small, small_bias, a_log, d_skip, conv_w, conv_b, gain, conv0, h0, layer, prevs,
        *, n_chunks, n, n_valid, out_dtype):
    depth_in, batch = h0.shape[:2]
    rows = SSD_HEADS * SSD_HEADDIM
    h0 = h0.reshape(depth_in, batch, rows, SSD_STATE)
    prevs = tuple(p.reshape(batch, rows, SSD_STATE) for p in prevs)
    h_in, h_prev, h_out, h_shape = _state_specs(h0, layer, prevs, (1, rows, SSD_STATE),
                                                lambda b, c: (b, 0, 0))
    alog = jnp.pad(a_log.astype(F32), (0, LANES - SSD_HEADS)).reshape(1, LANES)
    dskip = jnp.repeat(d_skip.astype(F32), SSD_HEADDIM).reshape(1, SSD_INNER)
    expand = (jnp.arange(LANES)[:, None] == (jnp.arange(SSD_INNER) // SSD_HEADDIM)[None, :]).astype(F32)
    conv0 = jnp.pad(conv0.astype(F32), ((0, 0), (CONV_ROWS - (SSD_CONV - 1), 0), (0, 0)))

    def tok(width):
        return pl.BlockSpec((n, width), lambda b, c: (b * n_chunks + c, 0))

    def const(shape):
        return pl.BlockSpec(shape, lambda b, c: (0,) * len(shape))

    y, h1 = pl.pallas_call(
        functools.partial(_ssd_kernel, n_valid=n_valid, n_chunks=n_chunks),
        grid=(batch, n_chunks),
        in_specs=[tok(SSD_XBC), tok(SSD_INNER), tok(LANES), const((1, LANES)), const((1, LANES)),
                  const((1, SSD_INNER)), const((SSD_CONV, SSD_XBC)), const((1, SSD_XBC)),
                  const((1, SSD_INNER)), const((LANES, SSD_INNER)),
                  pl.BlockSpec((1, CONV_ROWS, SSD_XBC), lambda b, c: (b, 0, 0)), h_in] + h_prev,
        out_specs=[tok(SSD_INNER), h_out],
        out_shape=[jax.ShapeDtypeStruct((batch * n_chunks * n, SSD_INNER), out_dtype), h_shape],
        scratch_shapes=[pltpu.VMEM((rows, SSD_STATE), F32), pltpu.VMEM((CONV_ROWS, SSD_XBC), F32),
                        pltpu.VMEM((n, SSD_XBC), F32), pltpu.VMEM((n, SSD_INNER), F32)],
        compiler_params=_params("parallel", "arbitrary"),
        name="ssd",
    )(sxbc, sz, small, small_bias, alog, dskip, conv_w.astype(F32), conv_b.reshape(1, SSD_XBC).astype(F32),
      gain.reshape(1, SSD_INNER).astype(F32), expand, conv0, h0, *prevs)
    return y, h1.reshape(h1.shape[:-2] + (SSD_HEADS, SSD_HEADDIM, SSD_STATE))


SAMPLE_ROWS = 8


def _rope_tables(pos):
    inv = ROPE_BASE ** (-jnp.arange(0, RET_DK, 2, dtype=F32) / RET_DK)
    ang = pos[:, None] * inv[None, :]
    return jnp.cos(ang), jnp.sin(ang)


def _col_slice(w, lo, hi):
    return w[:, IN_OFFSETS[lo]:IN_OFFSETS[hi]].astype(BF16)


def kernel(x_prompt, x_sample, state_ret, state_ssm, state_conv, state_mlstm_c, state_mlstm_n, state_mlstm_m, p_prompt, p_sample, norm_mix, w_in, ret_norm_gain, ssd_conv_w, ssd_conv_b, ssd_dt_bias, ssd_a_log, ssd_d, ssd_norm_gain, mlstm_b_i, mlstm_b_f, mlstm_norm_gain, w_branch_ret, w_branch_ssd, w_branch_mlstm, w_out, norm_ffn, peer_w_q, peer_keys1, peer_keys2, peer_u, peer_v, norm_ple, w_ple_gate, w_ple, norm_final):
    bp, tp, d = x_prompt.shape
    bs, ts, _ = x_sample.shape
    n_p = bp * tp
    n_s = bs * ts
    assert tp % CHUNK == 0 and SSD_CONV - 1 <= ts <= SAMPLE_ROWS
    h = jnp.concatenate([x_prompt.reshape(n_p, d), x_sample.reshape(n_s, d)], axis=0)

    cos_p, sin_p = _rope_tables(jnp.arange(tp, dtype=F32))
    cos_s, sin_s = _rope_tables(PAST_LEN + jnp.arange(SAMPLE_ROWS, dtype=F32))
    prompt_cfg = dict(n_chunks=tp // CHUNK, n=CHUNK, n_valid=CHUNK, out_dtype=BF16)
    sample_cfg = dict(n_chunks=1, n=SAMPLE_ROWS, n_valid=ts, out_dtype=F32)

    def sample_rows(a):
        a = a[n_p:].reshape(bs, ts, a.shape[-1])
        return jnp.pad(a, ((0, 0), (0, SAMPLE_ROWS - ts), (0, 0))).reshape(bs * SAMPLE_ROWS, a.shape[-1])

    def join(y_p, y_s):
        y_s = y_s.reshape(bs, SAMPLE_ROWS, -1)[:, :ts].reshape(n_s, -1).astype(BF16)
        return jnp.concatenate([y_p, y_s], axis=0)

    new_states = [[[] for _ in range(6)] for _ in range(2)]
    for l in range(DEPTH):
        u = rmsnorm([h], norm_mix[l], BF16)
        w = w_in[l]
        seg = [matmul(u, _col_slice(w, i, i + 1), name="in_proj")
               if IN_SIZES[i] % LANES == 0 else None for i in range(len(IN_SIZES))]
        small_w = jnp.concatenate([w[:, IN_OFFSETS[6]:IN_OFFSETS[7]], w[:, IN_OFFSETS[11]:IN_OFFSETS[13]]],
                                  axis=1)
        n_small = small_w.shape[1]
        small_w = jnp.pad(small_w, ((0, 0), (0, LANES - n_small))).astype(BF16)
        small = matmul(u, small_w, name="in_proj_small")
        small_bias = jnp.pad(jnp.concatenate([ssd_dt_bias[l], mlstm_b_i[l], mlstm_b_f[l]]).astype(F32),
                             (0, LANES - n_small)).reshape(1, LANES)
        rq, rk, rv, rg, sz, sxbc = seg[0:6]
        mq, mk, mv, mo = seg[7:11]
        gates = seg[13]

        yr_p, ret_p = retention(rq, rk, rv, rg, cos_p, sin_p, ret_norm_gain[l],
                                jnp.zeros((1, bp, RET_HEADS, RET_DK, RET_DV), F32), 0, (), **prompt_cfg)
        ys_p, ssm_p = ssd(sxbc, sz, small, small_bias, ssd_a_log[l], ssd_d[l], ssd_conv_w[l], ssd_conv_b[l],
                          ssd_norm_gain[l], jnp.zeros((bp, SSD_CONV - 1, SSD_XBC), F32),
                          jnp.zeros((1, bp, SSD_HEADS, SSD_HEADDIM, SSD_STATE), F32), 0, (), **prompt_cfg)
        ym_p, c_p, nn_p, m_p = mlstm(mq, mk, mv, mo, small, small_bias, mlstm_norm_gain[l],
                                     jnp.zeros((1, bp, MLSTM_HEADS, MLSTM_DK, MLSTM_DV), F32), 0, (),
                                     jnp.zeros((bp, MLSTM_HEADS, MLSTM_DK), F32),
                                     jnp.full((bp, MLSTM_HEADS), -jnp.inf, F32), **prompt_cfg)
        conv_p = sxbc[:n_p].reshape(bp, tp, SSD_XBC)[:, tp - (SSD_CONV - 1):]

        last = l == DEPTH - 1
        ret_prev, ssm_prev, _, c_prev = [tuple(slot) if last else () for slot in new_states[1][:4]]
        small_s = sample_rows(small)
        yr_s, ret_s = retention(sample_rows(rq), sample_rows(rk), sample_rows(rv), sample_rows(rg),
                                cos_s, sin_s, ret_norm_gain[l], state_ret, l, ret_prev, **sample_cfg)
        ys_s, ssm_s = ssd(sample_rows(sxbc), sample_rows(sz), small_s, small_bias, ssd_a_log[l], ssd_d[l],
                          ssd_conv_w[l], ssd_conv_b[l], ssd_norm_gain[l], state_conv[l], state_ssm, l,
                          ssm_prev, **sample_cfg)
        ym_s, c_s, nn_s, m_s = mlstm(sample_rows(mq), sample_rows(mk), sample_rows(mv), sample_rows(mo),
                                     small_s, small_bias, mlstm_norm_gain[l], state_mlstm_c, l, c_prev,
                                     state_mlstm_n[l], state_mlstm_m[l], **sample_cfg)
        conv_s = sxbc[n_p:].reshape(bs, ts, SSD_XBC)[:, ts - (SSD_CONV - 1):]

        for slot, val in zip(new_states[0], (ret_p, ssm_p, conv_p, c_p, nn_p, m_p)):
            slot.append(val)
        for slot, val in zip(new_states[1], (ret_s, ssm_s, conv_s, c_s, nn_s, m_s)):
            slot.append(val)
        y_r, y_s, y_m = join(yr_p, yr_s), join(ys_p, ys_s), join(ym_p, ym_s)

        gate_cols = D_MODEL // 1024
        merged = matmul(y_r, w_branch_ret[l].astype(BF16), epilogue=_gated_epilogue,
                        extras=((gates, 0),), name="branch_ret")
        merged = matmul(y_s, w_branch_ssd[l].astype(BF16), epilogue=_gated_add_epilogue,
                        extras=((gates, gate_cols), (merged, 0)), name="branch_ssd")
        merged = matmul(y_m, w_branch_mlstm[l].astype(BF16), out_dtype=BF16, epilogue=_gated_add_epilogue,
                        extras=((gates, 2 * gate_cols), (merged, 0)), name="branch_mlstm")
        h = matmul(merged, w_out[l].astype(BF16), epilogue=_residual_epilogue, extras=((h, 0),),
                   name="w_out")

        uf = rmsnorm([h], norm_ffn[l], BF16)
        q = matmul(uf, peer_w_q[l].astype(BF16), name="peer_q")
        s1, s2, e1, e2, thr = peer_scores(q, peer_keys1[l], peer_keys2[l])
        peer_out = peer_dense(uf, peer_u[l].astype(BF16), peer_v[l].astype(BF16), s1, s2, e1, e2, thr)

        up, h = rmsnorm([h, peer_out], norm_ple[l], BF16, emit_sum=True)
        pe = jnp.concatenate([p_prompt[l].reshape(n_p, -1), p_sample[l].reshape(n_s, -1)], axis=0)
        ple = matmul(pe.astype(BF16), w_ple[l].astype(BF16), name="ple")
        h = matmul(up, w_ple_gate[l].astype(BF16), epilogue=_ple_epilogue, extras=((h, 0), (ple, 0)),
                   name="ple_gate")

    y = rmsnorm([h], norm_final, F32)
    outs = [y[:n_p].reshape(bp, tp, d), y[n_p:].reshape(bs, ts, d)]
    outs.extend(jnp.stack(slot) for slot in new_states[0])
    stacked = (0, 1, 3)
    outs.extend(slot[-1] if i in stacked else jnp.stack(slot) for i, slot in enumerate(new_states[1]))
    return tuple(outs)
```
